```python
import jax, jax.numpy as jnp
from jax import lax
import numpy as np


D_MODEL = 4096
BATCH = 2
SEQ = 8192
DEPTH = 1

CHUNK = 64
HEAD_DIM = 128
N_HEADS_ATT = 16
N_HEADS_DN = 16
ATT_WIDTH = N_HEADS_ATT * HEAD_DIM
DN_WIDTH = N_HEADS_DN * HEAD_DIM
MIX_WIDTH = ATT_WIDTH + DN_WIDTH
LEFT_CHUNKS = 8
BAND = (LEFT_CHUNKS + 1) * CHUNK
REL_CLIP = 256
CONV_K = 4
N_GROUPS = 8
EXPERTS_PER_GROUP = 8
N_EXPERTS = N_GROUPS * EXPERTS_PER_GROUP
TOP_K_IN_GROUP = 2
D_EXPERT = 512
MOE_BLOCK = 128
EPS = 1e-6

OFF_ATT = 0
OFF_DN_QKV = OFF_ATT + 3 * ATT_WIDTH
OFF_DN_GATE = OFF_DN_QKV + 3 * DN_WIDTH
OFF_DN_BETA = OFF_DN_GATE + DN_WIDTH
OFF_DN_DECAY = OFF_DN_BETA + N_HEADS_DN
PROJ_COLS = OFF_DN_DECAY + N_HEADS_DN

kernel_name = 'hybrid_band_attn_gdn_hmoe'


def rms_norm(x, w):
    xf = x.astype(jnp.float32)
    y = xf * lax.rsqrt(jnp.mean(xf * xf, axis=-1, keepdims=True) + EPS)
    return (y * w.astype(jnp.float32)).astype(x.dtype)


def l2_norm(x):
    return x * lax.rsqrt(jnp.sum(x * x, axis=-1, keepdims=True) + EPS)


def causal_short_conv(x, w):
    s = x.shape[1]
    xp = jnp.pad(x, ((0, 0), (CONV_K - 1, 0), (0, 0)))
    y = xp[:, 0:s] * w[0]
    for i in range(1, CONV_K):
        y = y + xp[:, i:i + s] * w[i]
    return jax.nn.silu(y)


def band_attention(q, k, v, rel_bias):
    bsz, nh, s, dh = q.shape
    nc = s // CHUNK
    qc = jnp.moveaxis(q.reshape(bsz, nh, nc, CHUNK, dh), 2, 0)
    pad = ((0, 0), (0, 0), (LEFT_CHUNKS * CHUNK, 0), (0, 0))
    kp = jnp.pad(k, pad)
    vp = jnp.pad(v, pad)
    qi = jnp.arange(CHUNK)[:, None]
    kj = jnp.arange(BAND)
    rel = (LEFT_CHUNKS * CHUNK + qi) - kj[None, :]
    bias = rel_bias[:, jnp.clip(rel, -REL_CLIP, REL_CLIP) + REL_CLIP].astype(jnp.float32)
    scale = HEAD_DIM ** -0.5

    def one_chunk(inp):
        c, q_blk = inp
        k_blk = lax.dynamic_slice_in_dim(kp, c * CHUNK, BAND, axis=2)
        v_blk = lax.dynamic_slice_in_dim(vp, c * CHUNK, BAND, axis=2)
        sc = jnp.einsum('bhqd,bhkd->bhqk', q_blk, k_blk).astype(jnp.float32) * scale + bias
        valid = (c * CHUNK - LEFT_CHUNKS * CHUNK + kj) >= 0
        sc = jnp.where(valid, sc, -jnp.inf)
        pr = jax.nn.softmax(sc, axis=-1)
        return jnp.einsum('bhqk,bhkd->bhqd', pr.astype(v.dtype), v_blk)

    out = lax.map(one_chunk, (jnp.arange(nc), qc))
    return jnp.moveaxis(out, 0, 2).reshape(bsz, nh, s, dh)


def gated_delta_rule(q, k, v, g, beta):
    bsz, nh, s, dk = q.shape
    dv = v.shape[-1]
    nc = s // CHUNK
    q = q.reshape(bsz, nh, nc, CHUNK, dk)
    k = k.reshape(bsz, nh, nc, CHUNK, dk)
    v = v.reshape(bsz, nh, nc, CHUNK, dv)
    g = g.reshape(bsz, nh, nc, CHUNK)
    beta = beta.reshape(bsz, nh, nc, CHUNK)
    gc = jnp.cumsum(g, axis=-1)
    idx = jnp.arange(CHUNK)
    incl = idx[:, None] >= idx[None, :]
    strict = idx[:, None] > idx[None, :]
    gamma = jnp.exp(jnp.where(incl, gc[..., :, None] - gc[..., None, :], -jnp.inf))
    kbeta = k * beta[..., None]
    m = jnp.where(strict, jnp.einsum('bhnid,bhnjd->bhnij', kbeta, k) * gamma, 0.0)
    a = m + jnp.eye(CHUNK, dtype=m.dtype)
    rhs = jnp.concatenate([v * beta[..., None], kbeta * jnp.exp(gc)[..., None]], axis=-1)
    sol = lax.linalg.triangular_solve(a, rhs, left_side=True, lower=True, unit_diagonal=True)
    u = sol[..., :dv]
    w = sol[..., dv:]
    att = jnp.einsum('bhnid,bhnjd->bhnij', q, k) * gamma

    def step(state, inp):
        q_c, k_c, u_c, w_c, att_c, gc_c = inp
        v_new = u_c - jnp.einsum('bhck,bhkv->bhcv', w_c, state)
        o_c = (jnp.einsum('bhck,bhkv->bhcv', q_c * jnp.exp(gc_c)[..., None], state)
               + jnp.einsum('bhij,bhjv->bhiv', att_c, v_new))
        g_last = gc_c[..., -1]
        k_dec = k_c * jnp.exp(g_last[..., None] - gc_c)[..., None]
        state = state * jnp.exp(g_last)[..., None, None] + jnp.einsum('bhck,bhcv->bhkv', k_dec, v_new)
        return state, o_c

    xs = tuple(jnp.moveaxis(t, 2, 0) for t in (q, k, u, w, att, gc))
    state0 = jnp.zeros((bsz, nh, dk, dv), jnp.float32)
    _, o = lax.scan(step, state0, xs)
    return jnp.moveaxis(o, 0, 2).reshape(bsz, nh, s, dv)


def hier_moe(h, w_group, w_router, w_gate, w_up, w_down):
    bsz, s, d = h.shape
    t = bsz * s
    hf = h.reshape(t, d)
    grp_prob = jax.nn.softmax((hf @ w_group).astype(jnp.float32), axis=-1)
    grp_w, grp_idx = lax.top_k(grp_prob, 1)
    exp_logits = (hf @ w_router).astype(jnp.float32).reshape(t, N_GROUPS, EXPERTS_PER_GROUP)
    in_grp = jnp.take_along_axis(exp_logits, grp_idx[:, :, None], axis=1)[:, 0]
    top_w, top_i = lax.top_k(jax.nn.softmax(in_grp, axis=-1), TOP_K_IN_GROUP)
    top_w = top_w / jnp.sum(top_w, axis=-1, keepdims=True)
    gate = grp_w * top_w
    expert = grp_idx * EXPERTS_PER_GROUP + top_i

    n_assign = t * TOP_K_IN_GROUP
    flat_e = expert.reshape(n_assign)
    flat_tok = jnp.repeat(jnp.arange(t, dtype=jnp.int32), TOP_K_IN_GROUP)
    flat_g = gate.reshape(n_assign)
    order = jnp.argsort(flat_e)
    e_s = flat_e[order]
    tok_s = flat_tok[order]
    g_s = flat_g[order]
    counts = jax.ops.segment_sum(jnp.ones_like(flat_e), flat_e, num_segments=N_EXPERTS)
    padded = (counts + MOE_BLOCK - 1) // MOE_BLOCK * MOE_BLOCK
    pad_end = jnp.cumsum(padded)
    pad_start = pad_end - padded
    start = jnp.cumsum(counts) - counts
    dest = pad_start[e_s] + (jnp.arange(n_assign, dtype=jnp.int32) - start[e_s])
    n_slots = n_assign + N_EXPERTS * MOE_BLOCK
    n_blocks = n_slots // MOE_BLOCK
    buf_tok = jnp.zeros((n_slots,), jnp.int32).at[dest].set(tok_s)
    blk_e = jnp.minimum(jnp.searchsorted(pad_end, jnp.arange(n_blocks, dtype=jnp.int32) * MOE_BLOCK, side='right'), N_EXPERTS - 1)

    def expert_block(inp):
        tok, e = inp
        xb = hf[tok]
        hid = jax.nn.silu(xb @ w_gate[e]) * (xb @ w_up[e])
        return hid @ w_down[e]

    ys = lax.map(expert_block, (buf_tok.reshape(n_blocks, MOE_BLOCK), blk_e)).reshape(n_slots, d)
    out = jnp.zeros((t, d), h.dtype).at[tok_s].add(ys[dest] * g_s[:, None].astype(h.dtype))
    return out.reshape(bsz, s, d)


def hybrid_layer(x, norm1_w, w_in, q_norm_w, k_norm_w, rel_bias, conv_w, a_log, dt_bias,
                 o_norm_w, w_out, norm2_w, w_group, w_router, w_gate, w_up, w_down):
    bsz, s, _ = x.shape
    h = rms_norm(x, norm1_w)
    p = h @ w_in

    def heads(tn, nh):
        return tn.reshape(bsz, s, nh, HEAD_DIM).transpose(0, 2, 1, 3)

    qa = rms_norm(heads(p[..., OFF_ATT:OFF_ATT + ATT_WIDTH], N_HEADS_ATT), q_norm_w)
    ka = rms_norm(heads(p[..., OFF_ATT + ATT_WIDTH:OFF_ATT + 2 * ATT_WIDTH], N_HEADS_ATT), k_norm_w)
    va = heads(p[..., OFF_ATT + 2 * ATT_WIDTH:OFF_DN_QKV], N_HEADS_ATT)
    ya = band_attention(qa, ka, va, rel_bias).transpose(0, 2, 1, 3).reshape(bsz, s, ATT_WIDTH)

    qkv_b = causal_short_conv(p[..., OFF_DN_QKV:OFF_DN_GATE], conv_w).astype(jnp.float32)
    qb = l2_norm(heads(qkv_b[..., 0:DN_WIDTH], N_HEADS_DN)) * (HEAD_DIM ** -0.5)
    kb = l2_norm(heads(qkv_b[..., DN_WIDTH:2 * DN_WIDTH], N_HEADS_DN))
    vb = heads(qkv_b[..., 2 * DN_WIDTH:3 * DN_WIDTH], N_HEADS_DN)
    z = p[..., OFF_DN_GATE:OFF_DN_BETA].astype(jnp.float32).reshape(bsz, s, N_HEADS_DN, HEAD_DIM)
    beta = jax.nn.sigmoid(p[..., OFF_DN_BETA:OFF_DN_DECAY].astype(jnp.float32)).transpose(0, 2, 1)
    g = (-jnp.exp(a_log.astype(jnp.float32))
         * jax.nn.softplus(p[..., OFF_DN_DECAY:PROJ_COLS].astype(jnp.float32) + dt_bias.astype(jnp.float32))).transpose(0, 2, 1)
    ob = gated_delta_rule(qb, kb, vb, g, beta).transpose(0, 2, 1, 3)
    ob = rms_norm(ob, o_norm_w) * jax.nn.silu(z)
    yb = ob.reshape(bsz, s, DN_WIDTH).astype(x.dtype)

    x = x + jnp.concatenate([ya, yb], axis=-1) @ w_out
    x = x + hier_moe(rms_norm(x, norm2_w), w_group, w_router, w_gate, w_up, w_down)
    return x


def setup_inputs(seed: int = 0) -> dict:
    key = jax.random.key(seed)
    ks = jax.random.split(key, 20)
    f32 = jnp.float32
    nrm = lambda k, shape, sc: jax.random.normal(k, shape, f32) * sc
    dt = jnp.exp(jax.random.uniform(ks[8], (DEPTH, N_HEADS_DN), f32, np.log(1e-3), np.log(1e-1)))
    return {
        'x': jax.random.normal(ks[0], (BATCH, SEQ, D_MODEL), f32),
        'norm1_w': 1.0 + nrm(ks[1], (DEPTH, D_MODEL), 0.02),
        'w_in': nrm(ks[2], (DEPTH, D_MODEL, PROJ_COLS), D_MODEL ** -0.5),
        'q_norm_w': 1.0 + nrm(ks[3], (DEPTH, HEAD_DIM), 0.02),
        'k_norm_w': 1.0 + nrm(ks[4], (DEPTH, HEAD_DIM), 0.02),
        'rel_bias': nrm(ks[5], (DEPTH, N_HEADS_ATT, 2 * REL_CLIP + 1), 0.5),
        'conv_w': nrm(ks[6], (DEPTH, CONV_K, 3 * DN_WIDTH), CONV_K ** -0.5),
        'a_log': jnp.log(jax.random.uniform(ks[7], (DEPTH, N_HEADS_DN), f32, 1.0, 16.0)),
        'dt_bias': dt + jnp.log(-jnp.expm1(-dt)),
        'o_norm_w': 1.0 + nrm(ks[9], (DEPTH, HEAD_DIM), 0.02),
        'w_out': nrm(ks[10], (DEPTH, MIX_WIDTH, D_MODEL), MIX_WIDTH ** -0.5),
        'norm2_w': 1.0 + nrm(ks[11], (DEPTH, D_MODEL), 0.02),
        'w_group': nrm(ks[12], (DEPTH, D_MODEL, N_GROUPS), D_MODEL ** -0.5),
        'w_router': nrm(ks[13], (DEPTH, D_MODEL, N_EXPERTS), D_MODEL ** -0.5),
        'w_gate': nrm(ks[14], (DEPTH, N_EXPERTS, D_MODEL, D_EXPERT), D_MODEL ** -0.5),
        'w_up': nrm(ks[15], (DEPTH, N_EXPERTS, D_MODEL, D_EXPERT), D_MODEL ** -0.5),
        'w_down': nrm(ks[16], (DEPTH, N_EXPERTS, D_EXPERT, D_MODEL), D_EXPERT ** -0.5),
    }


def reference(x, norm1_w, w_in, q_norm_w, k_norm_w, rel_bias, conv_w, a_log, dt_bias,
              o_norm_w, w_out, norm2_w, w_group, w_router, w_gate, w_up, w_down):
    for l in range(DEPTH):
        x = hybrid_layer(x, norm1_w[l], w_in[l], q_norm_w[l], k_norm_w[l], rel_bias[l], conv_w[l],
                         a_log[l], dt_bias[l], o_norm_w[l], w_out[l], norm2_w[l], w_group[l],
                         w_router[l], w_gate[l], w_up[l], w_down[l])
    return x
```

```python
import functools

import jax
import jax.numpy as jnp
from jax import lax
from jax.experimental import pallas as pl
from jax.experimental.pallas import tpu as pltpu

F32 = jnp.float32
BF16 = jnp.bfloat16

CHUNK = 64
HEAD_DIM = 128
N_HEADS = 16
WIDTH = N_HEADS * HEAD_DIM
LEFT_CHUNKS = 8
REL_CLIP = 256
CONV_K = 4
N_GROUPS = 8
EXPERTS_PER_GROUP = 8
N_EXPERTS = N_GROUPS * EXPERTS_PER_GROUP
D_EXPERT = 512
EPS = 1e-6
NEG = -1e30

BLK_ATT_Q, BLK_ATT_K, BLK_ATT_V = 0, 16, 32
BLK_DN_Q, BLK_DN_K, BLK_DN_V, BLK_DN_Z = 48, 64, 80, 96
N_MAIN_BLKS = 112
MAIN_COLS = N_MAIN_BLKS * HEAD_DIM

SEQ_TILE = 512
MOE_BM = 256
COMBINE_TM = 256

V7X_VMEM_LIMIT = 56 * 1024 * 1024


def _cparams(sem, vmem=V7X_VMEM_LIMIT):
    return pltpu.CompilerParams(dimension_semantics=sem, vmem_limit_bytes=vmem)


def _rmsnorm_body(x_ref, w_ref, o_ref):
    x = x_ref[...]
    ms = jnp.mean(x * x, axis=-1, keepdims=True)
    o_ref[...] = (x * lax.rsqrt(ms + EPS) * w_ref[...]).astype(o_ref.dtype)


def _rmsnorm(x2d, w, out_dtype):
    t, d = x2d.shape
    tm = min(256, t)
    return pl.pallas_call(
        _rmsnorm_body,
        out_shape=jax.ShapeDtypeStruct((t, d), out_dtype),
        grid=(t // tm,),
        in_specs=[pl.BlockSpec((tm, d), lambda i: (i, 0)), pl.BlockSpec((1, d), lambda i: (0, 0))],
        out_specs=pl.BlockSpec((tm, d), lambda i: (i, 0)),
        compiler_params=_cparams(("parallel",)),
        name="rmsnorm",
    )(x2d, w.reshape(1, d))


def _inproj_body(h_ref, w_ref, o_ref):
    acc = jnp.dot(h_ref[...], w_ref[...].astype(BF16), preferred_element_type=F32)
    for c in range(o_ref.shape[0]):
        o_ref[c] = acc[:, c * HEAD_DIM:(c + 1) * HEAD_DIM].astype(o_ref.dtype)


def _inproj(h, w_in):
    t, d = h.shape
    tm = min(1024, t)
    tn = 512
    nb = tn // HEAD_DIM
    return pl.pallas_call(
        _inproj_body,
        out_shape=jax.ShapeDtypeStruct((N_MAIN_BLKS, t, HEAD_DIM), BF16),
        grid=(t // tm, MAIN_COLS // tn),
        in_specs=[pl.BlockSpec((tm, d), lambda i, j: (i, 0)), pl.BlockSpec((d, tn), lambda i, j: (0, j))],
        out_specs=pl.BlockSpec((nb, tm, HEAD_DIM), lambda i, j: (j, i, 0)),
        compiler_params=_cparams(("parallel", "arbitrary")),
        name="inproj",
    )(h, w_in)


def _chunk_cumsum_rows(x):
    row = lax.broadcasted_iota(jnp.int32, x.shape, 0) % CHUNK
    shift = 1
    while shift < CHUNK:
        x = x + jnp.where(row >= shift, pltpu.roll(x, shift, 0), 0.0)
        shift *= 2
    return x


def _gates_body(h_ref, w_ref, a_ref, dtb_ref, col_ref, row_ref):
    p = jnp.dot(h_ref[...], w_ref[...].astype(BF16), preferred_element_type=F32)
    lane = lax.broadcasted_iota(jnp.int32, p.shape, 1)
    beta = jax.nn.sigmoid(p)
    z = p + dtb_ref[...]
    softplus = jnp.maximum(z, 0.0) + jnp.log1p(jnp.exp(-jnp.abs(z)))
    g = -a_ref[...] * softplus
    out = jnp.where(lane < N_HEADS, beta, _chunk_cumsum_rows(g))
    col_ref[...] = out
    row_ref[...] = out.T[N_HEADS:2 * N_HEADS, :]


def _gates(h, w_small, a_log, dt_bias):
    t, d = h.shape
    tm = min(1024, t)
    pad = HEAD_DIM - 2 * N_HEADS
    w = jnp.pad(w_small, ((0, 0), (0, pad)))
    a = jnp.pad(jnp.exp(a_log.astype(F32)), (N_HEADS, pad)).reshape(1, HEAD_DIM)
    dtb = jnp.pad(dt_bias.astype(F32), (N_HEADS, pad)).reshape(1, HEAD_DIM)
    return pl.pallas_call(
        _gates_body,
        out_shape=(jax.ShapeDtypeStruct((t, HEAD_DIM), F32), jax.ShapeDtypeStruct((N_HEADS, t), F32)),
        grid=(t // tm,),
        in_specs=[
            pl.BlockSpec((tm, d), lambda i: (i, 0)),
            pl.BlockSpec((d, HEAD_DIM), lambda i: (0, 0)),
            pl.BlockSpec((1, HEAD_DIM), lambda i: (0, 0)),
            pl.BlockSpec((1, HEAD_DIM), lambda i: (0, 0)),
        ],
        out_specs=(pl.BlockSpec((tm, HEAD_DIM), lambda i: (i, 0)), pl.BlockSpec((N_HEADS, tm), lambda i: (0, i))),
        compiler_params=_cparams(("parallel",)),
        name="gates",
    )(h, w, a, dtb)


ATT_SUB = 128
ATT_WIN = ATT_SUB + LEFT_CHUNKS * CHUNK


def _attn_bias_tile(rel_bias):
    qi = jnp.arange(ATT_SUB)[:, None]
    kj = jnp.arange(ATT_WIN)[None, :]
    rel = LEFT_CHUNKS * CHUNK + qi - kj
    bias = rel_bias[:, jnp.clip(rel, -REL_CLIP, REL_CLIP) + REL_CLIP].astype(F32)
    qc = qi // CHUNK
    kc = kj // CHUNK
    allowed = (kc >= qc) & (kc <= qc + LEFT_CHUNKS)
    return jnp.where(allowed[None], bias, NEG)


def _head_rmsnorm(x, w):
    xf = x.astype(F32)
    return xf * lax.rsqrt(jnp.mean(xf * xf, axis=-1, keepdims=True) + EPS) * w


def _attn_body(q_ref, kp_ref, kc_ref, vp_ref, vc_ref, bias_ref, qw_ref, kw_ref, o_ref):
    t = pl.program_id(1)
    q = _head_rmsnorm(q_ref[...], qw_ref[...]).astype(BF16)
    k = jnp.concatenate(
        [_head_rmsnorm(kp_ref[...], kw_ref[...]), _head_rmsnorm(kc_ref[...], kw_ref[...])], axis=0
    ).astype(BF16)
    v = jnp.concatenate([vp_ref[...], vc_ref[...]], axis=0)
    bias = bias_ref[...]
    col = lax.broadcasted_iota(jnp.int32, (ATT_SUB, ATT_WIN), 1)
    for p in range(SEQ_TILE // ATT_SUB):
        lo = p * ATT_SUB
        s = lax.dot_general(
            q[lo:lo + ATT_SUB], k[lo:lo + ATT_WIN], (((1,), (1,)), ((), ())), preferred_element_type=F32
        ) + bias
        first_valid = jnp.where(t > 0, 0, SEQ_TILE - lo)
        s = jnp.where(col >= first_valid, s, NEG)
        m = jnp.max(s, axis=-1, keepdims=True)
        e = jnp.exp(s - m)
        l = jnp.sum(e, axis=-1, keepdims=True)
        o = jnp.dot(e.astype(BF16), v[lo:lo + ATT_WIN], preferred_element_type=F32) / l
        o_ref[lo:lo + ATT_SUB, :] = o.astype(o_ref.dtype)


def _band_attention(p3, bias_tile, q_norm_w, k_norm_w, bsz, seq):
    nt = seq // SEQ_TILE
    qw = (q_norm_w.astype(F32) * (HEAD_DIM ** -0.5)).reshape(1, HEAD_DIM)
    kw = k_norm_w.astype(F32).reshape(1, HEAD_DIM)

    def cur(blk):
        return lambda bh, t: (blk + bh % N_HEADS, (bh // N_HEADS) * nt + t, 0)

    def prev(blk):
        return lambda bh, t: (blk + bh % N_HEADS, (bh // N_HEADS) * nt + jnp.maximum(t - 1, 0), 0)

    tile = (None, SEQ_TILE, HEAD_DIM)
    return pl.pallas_call(
        _attn_body,
        out_shape=jax.ShapeDtypeStruct((bsz * seq, WIDTH), BF16),
        grid=(bsz * N_HEADS, nt),
        in_specs=[
            pl.BlockSpec(tile, cur(BLK_ATT_Q)),
            pl.BlockSpec(tile, prev(BLK_ATT_K)),
            pl.BlockSpec(tile, cur(BLK_ATT_K)),
            pl.BlockSpec(tile, prev(BLK_ATT_V)),
            pl.BlockSpec(tile, cur(BLK_ATT_V)),
            pl.BlockSpec((None, ATT_SUB, ATT_WIN), lambda bh, t: (bh % N_HEADS, 0, 0)),
            pl.BlockSpec((1, HEAD_DIM), lambda bh, t: (0, 0)),
            pl.BlockSpec((1, HEAD_DIM), lambda bh, t: (0, 0)),
        ],
        out_specs=pl.BlockSpec((SEQ_TILE, HEAD_DIM), lambda bh, t: ((bh // N_HEADS) * nt + t, bh % N_HEADS)),
        compiler_params=_cparams(("parallel", "arbitrary")),
        name="band_attention",
    )(p3, p3, p3, p3, p3, bias_tile, qw, kw)


CONV_HALO = 16


def _short_conv_silu(cur_ref, halo_ref, w_ref, t):
    cur = cur_ref[...].astype(F32)
    halo = jnp.where(t > 0, halo_ref[...].astype(F32), 0.0)
    cat = jnp.concatenate([halo, cur], axis=0)
    w = w_ref[...]
    y = cur * w[CONV_K - 1:CONV_K]
    for i in range(CONV_K - 1):
        shift = CONV_K - 1 - i
        y = y + pltpu.roll(cat, shift, 0)[CONV_HALO:] * w[i:i + 1]
    return y * jax.nn.sigmoid(y)


def _l2norm(x):
    return x * lax.rsqrt(jnp.sum(x * x, axis=-1, keepdims=True) + EPS)


def _bdot(a, b):
    return jnp.dot(a.astype(BF16), b.astype(BF16), preferred_element_type=F32)


def _bdot_nt(a, b):
    return lax.dot_general(a.astype(BF16), b.astype(BF16), (((1,), (1,)), ((), ())), preferred_element_type=F32)


def _gdn_body(q_ref, k_ref, v_ref, qh_ref, kh_ref, vh_ref, wq_ref, wk_ref, wv_ref, z_ref,
              gcol_ref, grow_ref, ow_ref, o_ref, state_ref):
    head = pl.program_id(0) % N_HEADS
    t = pl.program_id(1)

    @pl.when(t == 0)
    def _():
        state_ref[...] = jnp.zeros_like(state_ref)

    q = _l2norm(_short_conv_silu(q_ref, qh_ref, wq_ref, t)) * (HEAD_DIM ** -0.5)
    k = _l2norm(_short_conv_silu(k_ref, kh_ref, wk_ref, t))
    v = _short_conv_silu(v_ref, vh_ref, wv_ref, t)

    gates = gcol_ref[...]
    lane = lax.broadcasted_iota(jnp.int32, gates.shape, 1)
    beta = jnp.sum(jnp.where(lane == head, gates, 0.0), axis=-1, keepdims=True)
    gc = jnp.sum(jnp.where(lane == head + N_HEADS, gates, 0.0), axis=-1, keepdims=True)
    grow = grow_ref[...]

    ri = lax.broadcasted_iota(jnp.int32, (CHUNK, CHUNK), 0)
    ci = lax.broadcasted_iota(jnp.int32, (CHUNK, CHUNK), 1)
    eye = (ri == ci).astype(F32)

    state = state_ref[...]
    for c in range(SEQ_TILE // CHUNK):
        sl = slice(c * CHUNK, (c + 1) * CHUNK)
        qc, kc, vc, bc, gcc = q[sl], k[sl], v[sl], beta[sl], gc[sl]
        grc = grow[:, sl]
        gamma = jnp.exp(jnp.where(ri >= ci, gcc - grc, NEG))
        kb = kc * bc
        m = jnp.where(ri > ci, _bdot_nt(kb, kc) * gamma, 0.0)
        inv = eye - m
        pw = m
        for _ in range(5):
            pw = _bdot(pw, pw)
            inv = inv + _bdot(inv, pw)
        g_last = gcc[CHUNK - 1:CHUNK]
        rhs = jnp.concatenate([vc * bc, kb * jnp.exp(gcc)], axis=1)
        sol = _bdot(inv, rhs)
        u, w = sol[:, :HEAD_DIM], sol[:, HEAD_DIM:]
        att = _bdot_nt(qc, kc) * gamma
        v_new = u - _bdot(w, state)
        o_c = _bdot(qc * jnp.exp(gcc), state) + _bdot(att, v_new)
        k_dec = kc * jnp.exp(g_last - gcc)
        state = state * jnp.exp(g_last) + _bdot(k_dec.T, v_new)
        zc = z_ref[sl, :].astype(F32)
        o_n = o_c * lax.rsqrt(jnp.mean(o_c * o_c, axis=-1, keepdims=True) + EPS) * ow_ref[...]
        o_ref[sl, :] = (o_n * (zc * jax.nn.sigmoid(zc))).astype(o_ref.dtype)
    state_ref[...] = state


def _gated_delta(p3, conv_w, gcol, grow, o_norm_w, bsz, seq):
    nt = seq // SEQ_TILE
    halos_per_tile = SEQ_TILE // CONV_HALO
    cw = conv_w.astype(F32).reshape(CONV_K, 3 * N_HEADS, HEAD_DIM).transpose(1, 0, 2)
    grow3 = grow.reshape(N_HEADS, 1, bsz * seq)

    def cur(blk):
        return lambda bh, t: (blk + bh % N_HEADS, (bh // N_HEADS) * nt + t, 0)

    def halo(blk):
        return lambda bh, t: (
            blk + bh % N_HEADS, jnp.maximum(((bh // N_HEADS) * nt + t) * halos_per_tile - 1, 0), 0)

    def cwmap(off):
        return lambda bh, t: (off + bh % N_HEADS, 0, 0)

    tile = (None, SEQ_TILE, HEAD_DIM)
    htile = (None, CONV_HALO, HEAD_DIM)
    wtile = (None, CONV_K, HEAD_DIM)
    return pl.pallas_call(
        _gdn_body,
        out_shape=jax.ShapeDtypeStruct((bsz * seq, WIDTH), BF16),
        grid=(bsz * N_HEADS, nt),
        in_specs=[
            pl.BlockSpec(tile, cur(BLK_DN_Q)),
            pl.BlockSpec(tile, cur(BLK_DN_K)),
            pl.BlockSpec(tile, cur(BLK_DN_V)),
            pl.BlockSpec(htile, halo(BLK_DN_Q)),
            pl.BlockSpec(htile, halo(BLK_DN_K)),
            pl.BlockSpec(htile, halo(BLK_DN_V)),
            pl.BlockSpec(wtile, cwmap(0)),
            pl.BlockSpec(wtile, cwmap(N_HEADS)),
            pl.BlockSpec(wtile, cwmap(2 * N_HEADS)),
            pl.BlockSpec(tile, cur(BLK_DN_Z)),
            pl.BlockSpec((SEQ_TILE, HEAD_DIM), lambda bh, t: ((bh // N_HEADS) * nt + t, 0)),
            pl.BlockSpec((None, 1, SEQ_TILE), lambda bh, t: (bh % N_HEADS, 0, (bh // N_HEADS) * nt + t)),
            pl.BlockSpec((1, HEAD_DIM), lambda bh, t: (0, 0)),
        ],
        out_specs=pl.BlockSpec((SEQ_TILE, HEAD_DIM), lambda bh, t: ((bh // N_HEADS) * nt + t, bh % N_HEADS)),
        scratch_shapes=[pltpu.VMEM((HEAD_DIM, HEAD_DIM), F32)],
        compiler_params=_cparams(("parallel", "arbitrary")),
        name="gated_delta",
    )(p3, p3, p3, p3, p3, p3, cw, cw, cw, p3, gcol, grow3, o_norm_w.astype(F32).reshape(1, HEAD_DIM))


def _outproj_body(ya_ref, yb_ref, wa_ref, wb_ref, x_ref, o_ref):
    acc = jnp.dot(ya_ref[...], wa_ref[...].astype(BF16), preferred_element_type=F32)
    acc = acc + jnp.dot(yb_ref[...], wb_ref[...].astype(BF16), preferred_element_type=F32)
    o_ref[...] = x_ref[...] + acc


def _outproj(ya, yb, w_out, x2d):
    t, d = x2d.shape
    tm = min(1024, t)
    tn = min(512, d)
    return pl.pallas_call(
        _outproj_body,
        out_shape=jax.ShapeDtypeStruct((t, d), F32),
        grid=(t // tm, d // tn),
        in_specs=[
            pl.BlockSpec((tm, WIDTH), lambda i, j: (i, 0)),
            pl.BlockSpec((tm, WIDTH), lambda i, j: (i, 0)),
            pl.BlockSpec((WIDTH, tn), lambda i, j: (0, j)),
            pl.BlockSpec((WIDTH, tn), lambda i, j: (1, j)),
            pl.BlockSpec((tm, tn), lambda i, j: (i, j)),
        ],
        out_specs=pl.BlockSpec((tm, tn), lambda i, j: (i, j)),
        compiler_params=_cparams(("parallel", "arbitrary")),
        name="outproj",
    )(ya, yb, w_out, w_out, x2d)


def _router_body(x_ref, nw_ref, wr_ref, h_ref, r_ref):
    x = x_ref[...]
    h = x * lax.rsqrt(jnp.mean(x * x, axis=-1, keepdims=True) + EPS) * nw_ref[...]
    h_ref[...] = h
    logits = jnp.dot(h, wr_ref[...], precision=lax.Precision.HIGHEST, preferred_element_type=F32)
    lane = lax.broadcasted_iota(jnp.int32, logits.shape, 1).astype(F32)
    big = float(HEAD_DIM)

    def first_argmax(vals, vmax):
        return jnp.min(jnp.where(vals == vmax, lane, big), axis=-1, keepdims=True)

    gl = jnp.where(lane < N_GROUPS, logits, NEG)
    gmax = jnp.max(gl, axis=-1, keepdims=True)
    gidx = first_argmax(gl, gmax)
    grp_w = 1.0 / jnp.sum(jnp.exp(gl - gmax), axis=-1, keepdims=True)
    lo = N_GROUPS + gidx * EXPERTS_PER_GROUP
    el = jnp.where((lane >= lo) & (lane < lo + EXPERTS_PER_GROUP), logits, NEG)
    m1 = jnp.max(el, axis=-1, keepdims=True)
    i1 = first_argmax(el, m1)
    el2 = jnp.where(lane == i1, NEG, el)
    m2 = jnp.max(el2, axis=-1, keepdims=True)
    i2 = first_argmax(el2, m2)
    r = jnp.exp(m2 - m1)
    w1 = 1.0 / (1.0 + r)
    w2 = r / (1.0 + r)
    out = jnp.where(lane == 0, i1 - N_GROUPS, 0.0)
    out = jnp.where(lane == 1, i2 - N_GROUPS, out)
    out = jnp.where(lane == 2, grp_w * w1, out)
    out = jnp.where(lane == 3, grp_w * w2, out)
    r_ref[...] = out


def _norm_router(x1, norm_w, w_group, w_router):
    t, d = x1.shape
    tm = min(256, t)
    wr = jnp.concatenate([w_group, w_router], axis=1).astype(F32)
    wr = jnp.pad(wr, ((0, 0), (0, HEAD_DIM - wr.shape[1])))
    return pl.pallas_call(
        _router_body,
        out_shape=(jax.ShapeDtypeStruct((t, d), F32), jax.ShapeDtypeStruct((t, HEAD_DIM), F32)),
        grid=(t // tm,),
        in_specs=[
            pl.BlockSpec((tm, d), lambda i: (i, 0)),
            pl.BlockSpec((1, d), lambda i: (0, 0)),
            pl.BlockSpec((d, HEAD_DIM), lambda i: (0, 0)),
        ],
        out_specs=(pl.BlockSpec((tm, d), lambda i: (i, 0)), pl.BlockSpec((tm, HEAD_DIM), lambda i: (i, 0))),
        compiler_params=_cparams(("parallel",)),
        name="norm_router",
    )(x1, norm_w.astype(F32).reshape(1, d), wr)


def _row_gather_start(src_hbm, idx_ref, dst, sem, n_rows):
    def body(r, carry):
        pltpu.make_async_copy(src_hbm.at[pl.ds(idx_ref[0, r], 1)], dst.at[pl.ds(r, 1)], sem).start()
        return carry

    lax.fori_loop(0, n_rows, body, 0, unroll=8)


def _row_gather_wait(dst, sem):
    pltpu.make_async_copy(dst, dst, sem).wait()


def _ffn_body(nblk_ref, blke_ref, tok_ref, tokn_ref, x_hbm, wg_ref, wu_ref, wd_ref, y_ref,
              xbuf, wg_bf, wu_bf, wd_bf, sem):
    i = pl.program_id(0)
    nblk = nblk_ref[0]
    slot = i % 2

    @pl.when(jnp.logical_and(i == 0, nblk > 0))
    def _():
        _row_gather_start(x_hbm, tok_ref, xbuf.at[0], sem.at[0], MOE_BM)

    @pl.when(i + 1 < nblk)
    def _():
        _row_gather_start(x_hbm, tokn_ref, xbuf.at[1 - slot], sem.at[1 - slot], MOE_BM)

    new_expert = jnp.logical_or(i == 0, blke_ref[i] != blke_ref[jnp.maximum(i - 1, 0)])

    @pl.when(jnp.logical_and(i < nblk, new_expert))
    def _():
        wg_bf[...] = wg_ref[...].astype(BF16)
        wu_bf[...] = wu_ref[...].astype(BF16)
        wd_bf[...] = wd_ref[...].astype(BF16)

    @pl.when(i < nblk)
    def _():
        _row_gather_wait(xbuf.at[slot], sem.at[slot])
        x = xbuf[slot].astype(BF16)
        g = jnp.dot(x, wg_bf[...], preferred_element_type=F32)
        u = jnp.dot(x, wu_bf[...], preferred_element_type=F32)
        hid = (g * jax.nn.sigmoid(g) * u).astype(BF16)
        y_ref[...] = jnp.dot(hid, wd_bf[...], preferred_element_type=F32)

    @pl.when(i >= nblk)
    def _():
        y_ref[...] = jnp.zeros_like(y_ref)


def _expert_ffn(h2, w_gate, w_up, w_down, nblk, blk_e, slot_tok):
    t, d = h2.shape
    n_blocks = blk_e.shape[0]
    tok_arr = slot_tok.reshape(n_blocks, 1, MOE_BM)
    smem_blk = functools.partial(pl.BlockSpec, (None, 1, MOE_BM), memory_space=pltpu.SMEM)
    single = pl.Buffered(1)
    grid_spec = pltpu.PrefetchScalarGridSpec(
        num_scalar_prefetch=2,
        grid=(n_blocks,),
        in_specs=[
            smem_blk(lambda i, nb, be: (i, 0, 0)),
            smem_blk(lambda i, nb, be: (jnp.minimum(i + 1, n_blocks - 1), 0, 0)),
            pl.BlockSpec(memory_space=pl.ANY),
            pl.BlockSpec((None, d, D_EXPERT), lambda i, nb, be: (be[i], 0, 0), pipeline_mode=single),
            pl.BlockSpec((None, d, D_EXPERT), lambda i, nb, be: (be[i], 0, 0), pipeline_mode=single),
            pl.BlockSpec((None, D_EXPERT, d), lambda i, nb, be: (be[i], 0, 0), pipeline_mode=single),
        ],
        out_specs=pl.BlockSpec((MOE_BM, d), lambda i, nb, be: (i, 0)),
        scratch_shapes=[
            pltpu.VMEM((2, MOE_BM, d), F32),
            pltpu.VMEM((d, D_EXPERT), BF16),
            pltpu.VMEM((d, D_EXPERT), BF16),
            pltpu.VMEM((D_EXPERT, d), BF16),
            pltpu.SemaphoreType.DMA((2,)),
        ],
    )
    return pl.pallas_call(
        _ffn_body,
        out_shape=jax.ShapeDtypeStruct((n_blocks * MOE_BM, d), F32),
        grid_spec=grid_spec,
        compiler_params=_cparams(("arbitrary",)),
        name="expert_ffn",
    )(nblk, blk_e, tok_arr, tok_arr, h2, w_gate, w_up, w_down)


def _combine_body(d0_ref, d1_ref, d0n_ref, d1n_ref, x_ref, r_ref, y_hbm, o_ref, ybuf, sem):
    i = pl.program_id(0)
    n = pl.num_programs(0)
    slot = i % 2

    @pl.when(i == 0)
    def _():
        _row_gather_start(y_hbm, d0_ref, ybuf.at[0, 0], sem.at[0], COMBINE_TM)
        _row_gather_start(y_hbm, d1_ref, ybuf.at[0, 1], sem.at[0], COMBINE_TM)

    @pl.when(i + 1 < n)
    def _():
        _row_gather_start(y_hbm, d0n_ref, ybuf.at[1 - slot, 0], sem.at[1 - slot], COMBINE_TM)
        _row_gather_start(y_hbm, d1n_ref, ybuf.at[1 - slot, 1], sem.at[1 - slot], COMBINE_TM)

    _row_gather_wait(ybuf.at[slot], sem.at[slot])
    route = r_ref[...]
    o_ref[...] = x_ref[...] + route[:, 2:3] * ybuf[slot, 0] + route[:, 3:4] * ybuf[slot, 1]


def _combine(x1, route, y, dest0, dest1):
    t, d = x1.shape
    tm = min(COMBINE_TM, t)
    assert tm == COMBINE_TM
    n = t // tm
    d0 = dest0.reshape(n, 1, tm)
    d1 = dest1.reshape(n, 1, tm)
    smem_blk = functools.partial(pl.BlockSpec, (None, 1, tm), memory_space=pltpu.SMEM)
    nxt = lambda i: (jnp.minimum(i + 1, n - 1), 0, 0)
    return pl.pallas_call(
        _combine_body,
        out_shape=jax.ShapeDtypeStruct((t, d), F32),
        grid=(n,),
        in_specs=[
            smem_blk(lambda i: (i, 0, 0)),
            smem_blk(lambda i: (i, 0, 0)),
            smem_blk(nxt),
            smem_blk(nxt),
            pl.BlockSpec((tm, d), lambda i: (i, 0)),
            pl.BlockSpec((tm, HEAD_DIM), lambda i: (i, 0)),
            pl.BlockSpec(memory_space=pl.ANY),
        ],
        out_specs=pl.BlockSpec((tm, d), lambda i: (i, 0)),
        scratch_shapes=[pltpu.VMEM((2, 2, tm, d), F32), pltpu.SemaphoreType.DMA((2,))],
        compiler_params=_cparams(("arbitrary",)),
        name="moe_combine",
    )(d0, d1, d0, d1, x1, route, y)


def _dispatch_plan(route, t):
    expert = route[:, 0:2].astype(jnp.int32)
    n_assign = 2 * t
    flat_e = expert.reshape(n_assign)
    order = jnp.argsort(flat_e)
    e_s = flat_e[order]
    counts = jnp.zeros((N_EXPERTS,), jnp.int32).at[flat_e].add(1)
    padded = (counts + MOE_BM - 1) // MOE_BM * MOE_BM
    pad_end = jnp.cumsum(padded)
    pad_start = pad_end - padded
    start = jnp.cumsum(counts) - counts
    dest_sorted = pad_start[e_s] + (jnp.arange(n_assign, dtype=jnp.int32) - start[e_s])
    n_blocks = n_assign // MOE_BM + N_EXPERTS
    n_slots = n_blocks * MOE_BM
    slot_tok = jnp.zeros((n_slots,), jnp.int32).at[dest_sorted].set((order // 2).astype(jnp.int32))
    dest = jnp.zeros((n_assign,), jnp.int32).at[order].set(dest_sorted).reshape(t, 2)
    nblk = (pad_end[-1] // MOE_BM).astype(jnp.int32).reshape(1)
    blk_start = jnp.arange(n_blocks, dtype=jnp.int32) * MOE_BM
    blk_e = jnp.minimum(jnp.searchsorted(pad_end, blk_start, side="right"), N_EXPERTS - 1).astype(jnp.int32)
    last_e = blk_e[jnp.maximum(nblk[0] - 1, 0)]
    blk_e = jnp.where(jnp.arange(n_blocks) < nblk[0], blk_e, last_e)
    return nblk, blk_e, slot_tok, dest[:, 0], dest[:, 1]


def _layer(x, norm1_w, w_in, q_norm_w, k_norm_w, rel_bias, conv_w, a_log, dt_bias,
           o_norm_w, w_out, norm2_w, w_group, w_router, w_gate, w_up, w_down):
    bsz, seq, d = x.shape
    t = bsz * seq
    assert seq % SEQ_TILE == 0 and d % HEAD_DIM == 0 and t % COMBINE_TM == 0
    assert w_in.shape[1] == MAIN_COLS + 2 * N_HEADS
    x2d = x.reshape(t, d)

    h1 = _rmsnorm(x2d, norm1_w.astype(F32), BF16)
    p3 = _inproj(h1, w_in)
    gcol, grow = _gates(h1, w_in[:, MAIN_COLS:], a_log, dt_bias)

    ya = _band_attention(p3, _attn_bias_tile(rel_bias), q_norm_w, k_norm_w, bsz, seq)
    yb = _gated_delta(p3, conv_w, gcol, grow, o_norm_w, bsz, seq)
    x1 = _outproj(ya, yb, w_out, x2d)

    h2, route = _norm_router(x1, norm2_w, w_group, w_router)
    nblk, blk_e, slot_tok, dest0, dest1 = _dispatch_plan(route, t)
    y = _expert_ffn(h2, w_gate, w_up, w_down, nblk, blk_e, slot_tok)
    out = _combine(x1, route, y, dest0, dest1)
    return out.reshape(bsz, seq, d)


def kernel(x, norm1_w, w_in, q_norm_w, k_norm_w, rel_bias, conv_w, a_log, dt_bias, o_norm_w, w_out,
           norm2_w, w_group, w_router, w_gate, w_up, w_down):
    for l in range(norm1_w.shape[0]):
        x = _layer(x, norm1_w[l], w_in[l], q_norm_w[l], k_norm_w[l], rel_bias[l], conv_w[l], a_log[l],
                   dt_bias[l], o_norm_w[l], w_out[l], norm2_w[l], w_group[l], w_router[l], w_gate[l],
                   w_up[l], w_down[l])
    return x
```

```python
import functools

import jax
import jax.numpy as jnp
import numpy as np
from jax import lax
from jax.experimental import pallas as pl
from jax.experimental.pallas import tpu as pltpu

F32 = jnp.float32
BF16 = jnp.bfloat16

CHUNK = 64
HEAD_DIM = 128
N_HEADS = 16
WIDTH = N_HEADS * HEAD_DIM
LEFT_CHUNKS = 8
REL_CLIP = 256
CONV_K = 4
N_GROUPS = 8
EXPERTS_PER_GROUP = 8
N_EXPERTS = N_GROUPS * EXPERTS_PER_GROUP
D_EXPERT = 512
EPS = 1e-6
NEG = -1e30

BLK_ATT_Q, BLK_ATT_K, BLK_ATT_V = 0, 16, 32
BLK_DN_Q, BLK_DN_K, BLK_DN_V, BLK_DN_Z = 48, 64, 80, 96
N_MAIN_BLKS = 112
MAIN_COLS = N_MAIN_BLKS * HEAD_DIM

SEQ_TILE = 512
MOE_BM = 256
COMBINE_TM = 256

V7X_VMEM_LIMIT = 56 * 1024 * 1024


def _cparams(sem, vmem=V7X_VMEM_LIMIT):
    return pltpu.CompilerParams(dimension_semantics=sem, vmem_limit_bytes=vmem)


def _dot_nt(a, b):
    return lax.dot_general(a, b, (((1,), (1,)), ((), ())), preferred_element_type=F32)


def _rmsnorm_body(x_ref, w_ref, o_ref):
    x = x_ref[...]
    ms = jnp.mean(x * x, axis=-1, keepdims=True)
    o_ref[...] = (x * lax.rsqrt(ms + EPS) * w_ref[...]).astype(o_ref.dtype)


def _rmsnorm(x2d, w, out_dtype):
    t, d = x2d.shape
    tm = min(256, t)
    return pl.pallas_call(
        _rmsnorm_body,
        out_shape=jax.ShapeDtypeStruct((t, d), out_dtype),
        grid=(t // tm,),
        in_specs=[pl.BlockSpec((tm, d), lambda i: (i, 0)), pl.BlockSpec((1, d), lambda i: (0, 0))],
        out_specs=pl.BlockSpec((tm, d), lambda i: (i, 0)),
        compiler_params=_cparams(("parallel",)),
        name="rmsnorm",
    )(x2d, w.reshape(1, d))


def _inproj_body(h_ref, wt_ref, o_ref):
    acc = _dot_nt(h_ref[...], wt_ref[...].astype(BF16))
    for c in range(o_ref.shape[0]):
        o_ref[c] = acc[:, c * HEAD_DIM:(c + 1) * HEAD_DIM].astype(o_ref.dtype)


def _inproj(h, w_in_t):
    t, d = h.shape
    tm = min(1024, t)
    tn = 512
    nb = tn // HEAD_DIM
    return pl.pallas_call(
        _inproj_body,
        out_shape=jax.ShapeDtypeStruct((N_MAIN_BLKS, t, HEAD_DIM), BF16),
        grid=(t // tm, MAIN_COLS // tn),
        in_specs=[pl.BlockSpec((tm, d), lambda i, j: (i, 0)), pl.BlockSpec((tn, d), lambda i, j: (j, 0))],
        out_specs=pl.BlockSpec((nb, tm, HEAD_DIM), lambda i, j: (j, i, 0)),
        compiler_params=_cparams(("parallel", "arbitrary")),
        name="inproj",
    )(h, w_in_t)


def _chunk_cumsum_rows(x):
    row = lax.broadcasted_iota(jnp.int32, x.shape, 0) % CHUNK
    shift = 1
    while shift < CHUNK:
        x = x + jnp.where(row >= shift, pltpu.roll(x, shift, 0), 0.0)
        shift *= 2
    return x


def _gates_body(h_ref, wt_ref, a_ref, dtb_ref, col_ref, row_ref):
    p = _dot_nt(h_ref[...], wt_ref[...].astype(BF16))
    lane = lax.broadcasted_iota(jnp.int32, p.shape, 1)
    beta = jax.nn.sigmoid(p)
    z = p + dtb_ref[...]
    softplus = jnp.maximum(z, 0.0) + jnp.log1p(jnp.exp(-jnp.abs(z)))
    g = -a_ref[...] * softplus
    out = jnp.where(lane < N_HEADS, beta, _chunk_cumsum_rows(g))
    col_ref[...] = out
    row_ref[...] = out.T[N_HEADS:2 * N_HEADS, :]


def _gates(h, w_small_t, a_log, dt_bias):
    t, d = h.shape
    tm = min(1024, t)
    pad = HEAD_DIM - 2 * N_HEADS
    w = jnp.pad(w_small_t, ((0, pad), (0, 0)))
    a = jnp.pad(jnp.exp(a_log.astype(F32)), (N_HEADS, pad)).reshape(1, HEAD_DIM)
    dtb = jnp.pad(dt_bias.astype(F32), (N_HEADS, pad)).reshape(1, HEAD_DIM)
    return pl.pallas_call(
        _gates_body,
        out_shape=(jax.ShapeDtypeStruct((t, HEAD_DIM), F32), jax.ShapeDtypeStruct((N_HEADS, t), F32)),
        grid=(t // tm,),
        in_specs=[
            pl.BlockSpec((tm, d), lambda i: (i, 0)),
            pl.BlockSpec((HEAD_DIM, d), lambda i: (0, 0)),
            pl.BlockSpec((1, HEAD_DIM), lambda i: (0, 0)),
            pl.BlockSpec((1, HEAD_DIM), lambda i: (0, 0)),
        ],
        out_specs=(pl.BlockSpec((tm, HEAD_DIM), lambda i: (i, 0)), pl.BlockSpec((N_HEADS, tm), lambda i: (0, i))),
        compiler_params=_cparams(("parallel",)),
        name="gates",
    )(h, w, a, dtb)


ATT_SUB = 128
ATT_WIN = ATT_SUB + LEFT_CHUNKS * CHUNK


def _attn_bias_tile(rel_bias):
    period = ATT_SUB + ATT_WIN
    offs = np.arange(period)
    key_minus_query = np.where(offs < ATT_WIN, offs, offs - period)
    rel = LEFT_CHUNKS * CHUNK - key_minus_query
    diag_vals = rel_bias[:, np.clip(rel, -REL_CLIP, REL_CLIP) + REL_CLIP].astype(F32)
    flat = jnp.tile(diag_vals, (1, ATT_SUB))[:, :ATT_SUB * (period - 1)]
    bias = flat.reshape(-1, ATT_SUB, period - 1)[:, :, :ATT_WIN]
    qc = np.arange(ATT_SUB)[:, None] // CHUNK
    kc = np.arange(ATT_WIN)[None, :] // CHUNK
    allowed = (kc >= qc) & (kc <= qc + LEFT_CHUNKS)
    return jnp.where(allowed[None], bias, NEG)


def _head_rmsnorm(x, w):
    xf = x.astype(F32)
    return xf * lax.rsqrt(jnp.mean(xf * xf, axis=-1, keepdims=True) + EPS) * w


def _attn_body(q_ref, kp_ref, kc_ref, vp_ref, vc_ref, bias_ref, qw_ref, kw_ref, o_ref):
    t = pl.program_id(1)
    q = _head_rmsnorm(q_ref[...], qw_ref[...]).astype(BF16)
    k = jnp.concatenate(
        [_head_rmsnorm(kp_ref[...], kw_ref[...]), _head_rmsnorm(kc_ref[...], kw_ref[...])], axis=0
    ).astype(BF16)
    v = jnp.concatenate([vp_ref[...], vc_ref[...]], axis=0)
    bias = bias_ref[...]
    col = lax.broadcasted_iota(jnp.int32, (ATT_SUB, ATT_WIN), 1)
    for p in range(SEQ_TILE // ATT_SUB):
        lo = p * ATT_SUB
        s = lax.dot_general(
            q[lo:lo + ATT_SUB], k[lo:lo + ATT_WIN], (((1,), (1,)), ((), ())), preferred_element_type=F32
        ) + bias
        first_valid = jnp.where(t > 0, 0, SEQ_TILE - lo)
        s = jnp.where(col >= first_valid, s, NEG)
        m = jnp.max(s, axis=-1, keepdims=True)
        e = jnp.exp(s - m)
        l = jnp.sum(e, axis=-1, keepdims=True)
        o = jnp.dot(e.astype(BF16), v[lo:lo + ATT_WIN], preferred_element_type=F32) / l
        o_ref[lo:lo + ATT_SUB, :] = o.astype(o_ref.dtype)


def _band_attention(p3, bias_tile, q_norm_w, k_norm_w, bsz, seq):
    nt = seq // SEQ_TILE
    qw = (q_norm_w.astype(F32) * (HEAD_DIM ** -0.5)).reshape(1, HEAD_DIM)
    kw = k_norm_w.astype(F32).reshape(1, HEAD_DIM)

    def cur(blk):
        return lambda bh, t: (blk + bh % N_HEADS, (bh // N_HEADS) * nt + t, 0)

    def prev(blk):
        return lambda bh, t: (blk + bh % N_HEADS, (bh // N_HEADS) * nt + jnp.maximum(t - 1, 0), 0)

    tile = (None, SEQ_TILE, HEAD_DIM)
    return pl.pallas_call(
        _attn_body,
        out_shape=jax.ShapeDtypeStruct((bsz * seq, WIDTH), BF16),
        grid=(bsz * N_HEADS, nt),
        in_specs=[
            pl.BlockSpec(tile, cur(BLK_ATT_Q)),
            pl.BlockSpec(tile, prev(BLK_ATT_K)),
            pl.BlockSpec(tile, cur(BLK_ATT_K)),
            pl.BlockSpec(tile, prev(BLK_ATT_V)),
            pl.BlockSpec(tile, cur(BLK_ATT_V)),
            pl.BlockSpec((None, ATT_SUB, ATT_WIN), lambda bh, t: (bh % N_HEADS, 0, 0)),
            pl.BlockSpec((1, HEAD_DIM), lambda bh, t: (0, 0)),
            pl.BlockSpec((1, HEAD_DIM), lambda bh, t: (0, 0)),
        ],
        out_specs=pl.BlockSpec((SEQ_TILE, HEAD_DIM), lambda bh, t: ((bh // N_HEADS) * nt + t, bh % N_HEADS)),
        compiler_params=_cparams(("parallel", "arbitrary")),
        name="band_attention",
    )(p3, p3, p3, p3, p3, bias_tile, qw, kw)


CONV_HALO = 16


def _short_conv_silu(cur, halo, w, t):
    cur = cur.astype(F32)
    halo = jnp.where(t > 0, halo.astype(F32), 0.0)
    cat = jnp.concatenate([halo, cur], axis=0)
    y = cur * w[CONV_K - 1:CONV_K]
    for i in range(CONV_K - 1):
        shift = CONV_K - 1 - i
        y = y + pltpu.roll(cat, shift, 0)[CONV_HALO:] * w[i:i + 1]
    return y * jax.nn.sigmoid(y)


def _l2norm(x):
    return x * lax.rsqrt(jnp.sum(x * x, axis=-1, keepdims=True) + EPS)


def _bdot(a, b):
    return jnp.dot(a.astype(BF16), b.astype(BF16), preferred_element_type=F32)


def _bdot_nt(a, b):
    return lax.dot_general(a.astype(BF16), b.astype(BF16), (((1,), (1,)), ((), ())), preferred_element_type=F32)


GDN_HB = 4


def _bmm(a, b):
    return jnp.einsum("gmk,gkn->gmn", a.astype(BF16), b.astype(BF16), preferred_element_type=F32)


def _bmm_nt(a, b):
    return jnp.einsum("gmk,gnk->gmn", a.astype(BF16), b.astype(BF16), preferred_element_type=F32)


def _gdn_chunk_terms(q, k, v, beta, gc, grow):
    ri = lax.broadcasted_iota(jnp.int32, (1, CHUNK, CHUNK), 1)
    ci = lax.broadcasted_iota(jnp.int32, (1, CHUNK, CHUNK), 2)
    gamma = jnp.exp(jnp.where(ri >= ci, gc - grow, NEG))
    kb = k * beta
    kqk = _bmm_nt(jnp.concatenate([kb, q], axis=1), k)
    m = jnp.where(ri > ci, kqk[:, :CHUNK] * gamma, 0.0)
    att = kqk[:, CHUNK:] * gamma
    inv = (ri == ci).astype(F32) - m
    pw = m
    for _ in range(5):
        pw = _bmm(pw, pw)
        inv = inv + _bmm(inv, pw)
    eg = jnp.exp(gc)
    g_last = gc[:, CHUNK - 1:CHUNK]
    sol = _bmm(inv, jnp.concatenate([v * beta, kb * eg], axis=2))
    att_uw = _bmm(att, sol)
    k_dec = k * jnp.exp(g_last - gc)
    kd_uw = _bmm(jnp.swapaxes(k_dec, 1, 2), sol)
    lhs = jnp.concatenate([q * eg - att_uw[:, :, HEAD_DIM:], kd_uw[:, :, HEAD_DIM:]], axis=1)
    return lhs, att_uw[:, :, :HEAD_DIM], kd_uw[:, :, :HEAD_DIM], jnp.exp(g_last)


def _gdn_body(q_ref, k_ref, v_ref, qh_ref, kh_ref, vh_ref, wq_ref, wk_ref, wv_ref, z_ref,
              gcol_ref, grow_ref, ow_ref, o_ref, state_ref):
    group = pl.program_id(0) % (N_HEADS // GDN_HB)
    t = pl.program_id(1)

    @pl.when(t == 0)
    def _():
        state_ref[...] = jnp.zeros_like(state_ref)

    gates = gcol_ref[...]
    lane = lax.broadcasted_iota(jnp.int32, gates.shape, 1)
    n_chunks = SEQ_TILE // CHUNK
    n_prob = GDN_HB * n_chunks

    qs, ks, vs, betas, gcs, grows = [], [], [], [], [], []
    for hh in range(GDN_HB):
        head = group * GDN_HB + hh
        qs.append(_l2norm(_short_conv_silu(q_ref[hh], qh_ref[hh], wq_ref[hh], t)) * (HEAD_DIM ** -0.5))
        ks.append(_l2norm(_short_conv_silu(k_ref[hh], kh_ref[hh], wk_ref[hh], t)))
        vs.append(_short_conv_silu(v_ref[hh], vh_ref[hh], wv_ref[hh], t))
        betas.append(jnp.sum(jnp.where(lane == head, gates, 0.0), axis=-1, keepdims=True))
        gcs.append(jnp.sum(jnp.where(lane == head + N_HEADS, gates, 0.0), axis=-1, keepdims=True))
        grow = grow_ref[hh]
        grows.extend(grow[:, c * CHUNK:(c + 1) * CHUNK] for c in range(n_chunks))

    def chunked(parts):
        return jnp.stack(parts, axis=0).reshape(n_prob, CHUNK, parts[0].shape[-1])

    lhs, o_intra, s_add, decay = _gdn_chunk_terms(
        chunked(qs), chunked(ks), chunked(vs), chunked(betas), chunked(gcs), jnp.stack(grows, axis=0))
    lhs = lhs.reshape(GDN_HB, n_chunks, CHUNK + HEAD_DIM, HEAD_DIM)
    o_intra = o_intra.reshape(GDN_HB, n_chunks, CHUNK, HEAD_DIM)
    s_add = s_add.reshape(GDN_HB, n_chunks, HEAD_DIM, HEAD_DIM)
    decay = decay.reshape(GDN_HB, n_chunks, 1, 1)

    state = state_ref[...]
    ow = ow_ref[...]
    for c in range(n_chunks):
        prod = _bmm(lhs[:, c], state)
        o_c = prod[:, :CHUNK] + o_intra[:, c]
        state = state * decay[:, c] - prod[:, CHUNK:] + s_add[:, c]
        o_n = o_c * lax.rsqrt(jnp.mean(o_c * o_c, axis=-1, keepdims=True) + EPS) * ow
        for hh in range(GDN_HB):
            zc = z_ref[hh, c * CHUNK:(c + 1) * CHUNK, :].astype(F32)
            o_ref[c * CHUNK:(c + 1) * CHUNK, hh * HEAD_DIM:(hh + 1) * HEAD_DIM] = (
                o_n[hh] * (zc * jax.nn.sigmoid(zc))).astype(o_ref.dtype)
    state_ref[...] = state


def _gated_delta(p3, conv_w, gcol, grow, o_norm_w, bsz, seq):
    nt = seq // SEQ_TILE
    halos_per_tile = SEQ_TILE // CONV_HALO
    n_groups = N_HEADS // GDN_HB
    cw = conv_w.astype(F32).reshape(CONV_K, 3 * N_HEADS, HEAD_DIM).transpose(1, 0, 2)
    grow3 = grow.reshape(N_HEADS, 1, bsz * seq)

    def cur(blk):
        return lambda bg, t: (blk // GDN_HB + bg % n_groups, (bg // n_groups) * nt + t, 0)

    def halo(blk):
        return lambda bg, t: (
            blk // GDN_HB + bg % n_groups, jnp.maximum(((bg // n_groups) * nt + t) * halos_per_tile - 1, 0), 0)

    def cwmap(off):
        return lambda bg, t: (off // GDN_HB + bg % n_groups, 0, 0)

    tile = (GDN_HB, SEQ_TILE, HEAD_DIM)
    htile = (GDN_HB, CONV_HALO, HEAD_DIM)
    wtile = (GDN_HB, CONV_K, HEAD_DIM)
    return pl.pallas_call(
        _gdn_body,
        out_shape=jax.ShapeDtypeStruct((bsz * seq, WIDTH), BF16),
        grid=(bsz * n_groups, nt),
        in_specs=[
            pl.BlockSpec(tile, cur(BLK_DN_Q)),
            pl.BlockSpec(tile, cur(BLK_DN_K)),
            pl.BlockSpec(tile, cur(BLK_DN_V)),
            pl.BlockSpec(htile, halo(BLK_DN_Q)),
            pl.BlockSpec(htile, halo(BLK_DN_K)),
            pl.BlockSpec(htile, halo(BLK_DN_V)),
            pl.BlockSpec(wtile, cwmap(0)),
            pl.BlockSpec(wtile, cwmap(N_HEADS)),
            pl.BlockSpec(wtile, cwmap(2 * N_HEADS)),
            pl.BlockSpec(tile, cur(BLK_DN_Z)),
            pl.BlockSpec((SEQ_TILE, HEAD_DIM), lambda bg, t: ((bg // n_groups) * nt + t, 0)),
            pl.BlockSpec((GDN_HB, 1, SEQ_TILE), lambda bg, t: (bg % n_groups, 0, (bg // n_groups) * nt + t)),
            pl.BlockSpec((1, HEAD_DIM), lambda bg, t: (0, 0)),
        ],
        out_specs=pl.BlockSpec(
            (SEQ_TILE, GDN_HB * HEAD_DIM), lambda bg, t: ((bg // n_groups) * nt + t, bg % n_groups)),
        scratch_shapes=[pltpu.VMEM((GDN_HB, HEAD_DIM, HEAD_DIM), F32)],
        compiler_params=_cparams(("parallel", "arbitrary")),
        name="gated_delta",
    )(p3, p3, p3, p3, p3, p3, cw, cw, cw, p3, gcol, grow3, o_norm_w.astype(F32).reshape(1, HEAD_DIM))


def _outproj_body(ya_ref, yb_ref, wa_ref, wb_ref, x_ref, o_ref):
    acc = jnp.dot(ya_ref[...], wa_ref[...].astype(BF16), preferred_element_type=F32)
    acc = acc + jnp.dot(yb_ref[...], wb_ref[...].astype(BF16), preferred_element_type=F32)
    o_ref[...] = x_ref[...] + acc


def _outproj(ya, yb, w_out, x2d):
    t, d = x2d.shape
    tm = min(1024, t)
    tn = min(512, d)
    return pl.pallas_call(
        _outproj_body,
        out_shape=jax.ShapeDtypeStruct((t, d), F32),
        grid=(t // tm, d // tn),
        in_specs=[
            pl.BlockSpec((tm, WIDTH), lambda i, j: (i, 0)),
            pl.BlockSpec((tm, WIDTH), lambda i, j: (i, 0)),
            pl.BlockSpec((WIDTH, tn), lambda i, j: (0, j)),
            pl.BlockSpec((WIDTH, tn), lambda i, j: (1, j)),
            pl.BlockSpec((tm, tn), lambda i, j: (i, j)),
        ],
        out_specs=pl.BlockSpec((tm, tn), lambda i, j: (i, j)),
        compiler_params=_cparams(("parallel", "arbitrary")),
        name="outproj",
    )(ya, yb, w_out, w_out, x2d)


def _router_body(x_ref, nw_ref, wr_ref, h_ref, r_ref):
    x = x_ref[...]
    h = x * lax.rsqrt(jnp.mean(x * x, axis=-1, keepdims=True) + EPS) * nw_ref[...]
    h_ref[...] = h
    logits = jnp.dot(h, wr_ref[...], precision=lax.Precision.HIGHEST, preferred_element_type=F32)
    lane = lax.broadcasted_iota(jnp.int32, logits.shape, 1).astype(F32)
    big = float(HEAD_DIM)

    def first_argmax(vals, vmax):
        return jnp.min(jnp.where(vals == vmax, lane, big), axis=-1, keepdims=True)

    gl = jnp.where(lane < N_GROUPS, logits, NEG)
    gmax = jnp.max(gl, axis=-1, keepdims=True)
    gidx = first_argmax(gl, gmax)
    grp_w = 1.0 / jnp.sum(jnp.exp(gl - gmax), axis=-1, keepdims=True)
    lo = N_GROUPS + gidx * EXPERTS_PER_GROUP
    el = jnp.where((lane >= lo) & (lane < lo + EXPERTS_PER_GROUP), logits, NEG)
    m1 = jnp.max(el, axis=-1, keepdims=True)
    i1 = first_argmax(el, m1)
    el2 = jnp.where(lane == i1, NEG, el)
    m2 = jnp.max(el2, axis=-1, keepdims=True)
    i2 = first_argmax(el2, m2)
    r = jnp.exp(m2 - m1)
    w1 = 1.0 / (1.0 + r)
    w2 = r / (1.0 + r)
    out = jnp.where(lane == 0, i1 - N_GROUPS, 0.0)
    out = jnp.where(lane == 1, i2 - N_GROUPS, out)
    out = jnp.where(lane == 2, grp_w * w1, out)
    out = jnp.where(lane == 3, grp_w * w2, out)
    r_ref[...] = out


def _norm_router(x1, norm_w, w_group, w_router):
    t, d = x1.shape
    tm = min(256, t)
    wr = jnp.concatenate([w_group, w_router], axis=1).astype(F32)
    wr = jnp.pad(wr, ((0, 0), (0, HEAD_DIM - wr.shape[1])))
    return pl.pallas_call(
        _router_body,
        out_shape=(jax.ShapeDtypeStruct((t, d), F32), jax.ShapeDtypeStruct((t, HEAD_DIM), F32)),
        grid=(t // tm,),
        in_specs=[
            pl.BlockSpec((tm, d), lambda i: (i, 0)),
            pl.BlockSpec((1, d), lambda i: (0, 0)),
            pl.BlockSpec((d, HEAD_DIM), lambda i: (0, 0)),
        ],
        out_specs=(pl.BlockSpec((tm, d), lambda i: (i, 0)), pl.BlockSpec((tm, HEAD_DIM), lambda i: (i, 0))),
        compiler_params=_cparams(("parallel",)),
        name="norm_router",
    )(x1, norm_w.astype(F32).reshape(1, d), wr)


def _row_gather_start(src_hbm, idx_ref, dst, sem, n_rows):
    def body(r, carry):
        pltpu.make_async_copy(src_hbm.at[pl.ds(idx_ref[0, r], 1)], dst.at[pl.ds(r, 1)], sem).start()
        return carry

    lax.fori_loop(0, n_rows, body, 0, unroll=8)


def _row_gather_wait(dst, sem):
    pltpu.make_async_copy(dst, dst, sem).wait()


def _ffn_body(nblk_ref, blke_ref, tok_ref, tokn_ref, x_hbm, wg_ref, wu_ref, wd_ref, y_ref,
              xbuf, wg_bf, wu_bf, wd_bf, sem):
    i = pl.program_id(0)
    nblk = nblk_ref[0]
    slot = i % 2

    @pl.when(jnp.logical_and(i == 0, nblk > 0))
    def _():
        _row_gather_start(x_hbm, tok_ref, xbuf.at[0], sem.at[0], MOE_BM)

    @pl.when(i + 1 < nblk)
    def _():
        _row_gather_start(x_hbm, tokn_ref, xbuf.at[1 - slot], sem.at[1 - slot], MOE_BM)

    new_expert = jnp.logical_or(i == 0, blke_ref[i] != blke_ref[jnp.maximum(i - 1, 0)])

    @pl.when(jnp.logical_and(i < nblk, new_expert))
    def _():
        wg_bf[...] = wg_ref[...].astype(BF16)
        wu_bf[...] = wu_ref[...].astype(BF16)
        wd_bf[...] = wd_ref[...].astype(BF16)

    @pl.when(i < nblk)
    def _():
        _row_gather_wait(xbuf.at[slot], sem.at[slot])
        x = xbuf[slot].astype(BF16)
        g = jnp.dot(x, wg_bf[...], preferred_element_type=F32)
        u = jnp.dot(x, wu_bf[...], preferred_element_type=F32)
        hid = (g * jax.nn.sigmoid(g) * u).astype(BF16)
        y_ref[...] = jnp.dot(hid, wd_bf[...], preferred_element_type=F32)

    @pl.when(i >= nblk)
    def _():
        y_ref[...] = jnp.zeros_like(y_ref)


def _expert_ffn(h2, w_gate, w_up, w_down, nblk, blk_e, slot_tok):
    t, d = h2.shape
    n_blocks = blk_e.shape[0]
    tok_arr = slot_tok.reshape(n_blocks, 1, MOE_BM)
    smem_blk = functools.partial(pl.BlockSpec, (None, 1, MOE_BM), memory_space=pltpu.SMEM)
    single = pl.Buffered(1)
    grid_spec = pltpu.PrefetchScalarGridSpec(
        num_scalar_prefetch=2,
        grid=(n_blocks,),
        in_specs=[
            smem_blk(lambda i, nb, be: (i, 0, 0)),
            smem_blk(lambda i, nb, be: (jnp.minimum(i + 1, n_blocks - 1), 0, 0)),
            pl.BlockSpec(memory_space=pl.ANY),
            pl.BlockSpec((None, d, D_EXPERT), lambda i, nb, be: (be[i], 0, 0), pipeline_mode=single),
            pl.BlockSpec((None, d, D_EXPERT), lambda i, nb, be: (be[i], 0, 0), pipeline_mode=single),
            pl.BlockSpec((None, D_EXPERT, d), lambda i, nb, be: (be[i], 0, 0), pipeline_mode=single),
        ],
        out_specs=pl.BlockSpec((MOE_BM, d), lambda i, nb, be: (i, 0)),
        scratch_shapes=[
            pltpu.VMEM((2, MOE_BM, d), F32),
            pltpu.VMEM((d, D_EXPERT), BF16),
            pltpu.VMEM((d, D_EXPERT), BF16),
            pltpu.VMEM((D_EXPERT, d), BF16),
            pltpu.SemaphoreType.DMA((2,)),
        ],
    )
    return pl.pallas_call(
        _ffn_body,
        out_shape=jax.ShapeDtypeStruct((n_blocks * MOE_BM, d), F32),
        grid_spec=grid_spec,
        compiler_params=_cparams(("arbitrary",)),
        name="expert_ffn",
    )(nblk, blk_e, tok_arr, tok_arr, h2, w_gate, w_up, w_down)


def _combine_body(d0_ref, d1_ref, d0n_ref, d1n_ref, x_ref, r_ref, y_hbm, o_ref, ybuf, sem):
    i = pl.program_id(0)
    n = pl.num_programs(0)
    slot = i % 2

    @pl.when(i == 0)
    def _():
        _row_gather_start(y_hbm, d0_ref, ybuf.at[0, 0], sem.at[0], COMBINE_TM)
        _row_gather_start(y_hbm, d1_ref, ybuf.at[0, 1], sem.at[0], COMBINE_TM)

    @pl.when(i + 1 < n)
    def _():
        _row_gather_start(y_hbm, d0n_ref, ybuf.at[1 - slot, 0], sem.at[1 - slot], COMBINE_TM)
        _row_gather_start(y_hbm, d1n_ref, ybuf.at[1 - slot, 1], sem.at[1 - slot], COMBINE_TM)

    _row_gather_wait(ybuf.at[slot], sem.at[slot])
    route = r_ref[...]
    o_ref[...] = x_ref[...] + route[:, 2:3] * ybuf[slot, 0] + route[:, 3:4] * ybuf[slot, 1]


def _combine(x1, route, y, dest0, dest1):
    t, d = x1.shape
    tm = min(COMBINE_TM, t)
    assert tm == COMBINE_TM
    n = t // tm
    d0 = dest0.reshape(n, 1, tm)
    d1 = dest1.reshape(n, 1, tm)
    smem_blk = functools.partial(pl.BlockSpec, (None, 1, tm), memory_space=pltpu.SMEM)
    nxt = lambda i: (jnp.minimum(i + 1, n - 1), 0, 0)
    return pl.pallas_call(
        _combine_body,
        out_shape=jax.ShapeDtypeStruct((t, d), F32),
        grid=(n,),
        in_specs=[
            smem_blk(lambda i: (i, 0, 0)),
            smem_blk(lambda i: (i, 0, 0)),
            smem_blk(nxt),
            smem_blk(nxt),
            pl.BlockSpec((tm, d), lambda i: (i, 0)),
            pl.BlockSpec((tm, HEAD_DIM), lambda i: (i, 0)),
            pl.BlockSpec(memory_space=pl.ANY),
        ],
        out_specs=pl.BlockSpec((tm, d), lambda i: (i, 0)),
        scratch_shapes=[pltpu.VMEM((2, 2, tm, d), F32), pltpu.SemaphoreType.DMA((2,))],
        compiler_params=_cparams(("arbitrary",)),
        name="moe_combine",
    )(d0, d1, d0, d1, x1, route, y)


def _dispatch_plan(route, t):
    expert = route[:, 0:2].astype(jnp.int32)
    n_assign = 2 * t
    flat_e = expert.reshape(n_assign)
    order = jnp.argsort(flat_e).astype(jnp.int32)
    e_s = flat_e[order]
    below = jnp.sum(flat_e[None, :] < jnp.arange(N_EXPERTS + 1, dtype=jnp.int32)[:, None], axis=1)
    start = below[:-1].astype(jnp.int32)
    counts = (below[1:] - below[:-1]).astype(jnp.int32)
    padded = (counts + MOE_BM - 1) // MOE_BM * MOE_BM
    pad_end = jnp.cumsum(padded)
    pad_start = pad_end - padded
    dest_sorted = pad_start[e_s] + (jnp.arange(n_assign, dtype=jnp.int32) - start[e_s])
    _, dest = lax.sort_key_val(order, dest_sorted)
    dest = dest.reshape(t, 2)
    n_blocks = n_assign // MOE_BM + N_EXPERTS
    nblk = (pad_end[-1] // MOE_BM).astype(jnp.int32).reshape(1)
    blk_start = jnp.arange(n_blocks, dtype=jnp.int32) * MOE_BM
    blk_e = jnp.minimum(
        jnp.sum(pad_end[None, :] <= blk_start[:, None], axis=1), N_EXPERTS - 1).astype(jnp.int32)
    slot_e = jnp.repeat(blk_e, MOE_BM)
    off = jnp.arange(n_blocks * MOE_BM, dtype=jnp.int32) - pad_start[slot_e]
    src = jnp.clip(start[slot_e] + off, 0, n_assign - 1)
    slot_tok = jnp.where(off < counts[slot_e], order[src] // 2, 0).astype(jnp.int32)
    last_e = blk_e[jnp.maximum(nblk[0] - 1, 0)]
    blk_e = jnp.where(jnp.arange(n_blocks) < nblk[0], blk_e, last_e)
    return nblk, blk_e, slot_tok, dest[:, 0], dest[:, 1]


def _layer(x, norm1_w, w_in, q_norm_w, k_norm_w, rel_bias, conv_w, a_log, dt_bias,
           o_norm_w, w_out, norm2_w, w_group, w_router, w_gate, w_up, w_down):
    bsz, seq, d = x.shape
    t = bsz * seq
    assert seq % SEQ_TILE == 0 and d % HEAD_DIM == 0 and t % COMBINE_TM == 0
    assert w_in.shape[1] == MAIN_COLS + 2 * N_HEADS
    x2d = x.reshape(t, d)

    w_in_t = jnp.swapaxes(w_in, 0, 1)
    h1 = _rmsnorm(x2d, norm1_w.astype(F32), BF16)
    p3 = _inproj(h1, w_in_t)
    gcol, grow = _gates(h1, w_in_t[MAIN_COLS:], a_log, dt_bias)

    ya = _band_attention(p3, _attn_bias_tile(rel_bias), q_norm_w, k_norm_w, bsz, seq)
    yb = _gated_delta(p3, conv_w, gcol, grow, o_norm_w, bsz, seq)
    x1 = _outproj(ya, yb, w_out, x2d)

    h2, route = _norm_router(x1, norm2_w, w_group, w_router)
    nblk, blk_e, slot_tok, dest0, dest1 = _dispatch_plan(route, t)
    y = _expert_ffn(h2, w_gate, w_up, w_down, nblk, blk_e, slot_tok)
    out = _combine(x1, route, y, dest0, dest1)
    return out.reshape(bsz, seq, d)


def kernel(x, norm1_w, w_in, q_norm_w, k_norm_w, rel_bias, conv_w, a_log, dt_bias, o_norm_w, w_out,
           norm2_w, w_group, w_router, w_gate, w_up, w_down):
    for l in range(norm1_w.shape[0]):
        x = _layer(x, norm1_w[l], w_in[l], q_norm_w[l], k_norm_w[l], rel_bias[l], conv_w[l], a_log[l],
                   dt_bias[l], o_norm_w[l], w_out[l], norm2_w[l], w_group[l], w_router[l], w_gate[l],
                   w_up[l], w_down[l])
    return x
```

```python
import functools

import jax
import jax.numpy as jnp
import numpy as np
from jax import lax
from jax.experimental import pallas as pl
from jax.experimental.pallas import tpu as pltpu

F32 = jnp.float32
BF16 = jnp.bfloat16

CHUNK = 64
HEAD_DIM = 128
N_HEADS = 16
WIDTH = N_HEADS * HEAD_DIM
LEFT_CHUNKS = 8
REL_CLIP = 256
CONV_K = 4
N_GROUPS = 8
EXPERTS_PER_GROUP = 8
N_EXPERTS = N_GROUPS * EXPERTS_PER_GROUP
D_EXPERT = 512
EPS = 1e-6
NEG = -1e30

BLK_ATT_Q, BLK_ATT_K, BLK_ATT_V = 0, 16, 32
BLK_DN_Q, BLK_DN_K, BLK_DN_V, BLK_DN_Z = 48, 64, 80, 96
N_MAIN_BLKS = 112
MAIN_COLS = N_MAIN_BLKS * HEAD_DIM

SEQ_TILE = 512
COMBINE_TM = 256

V7X_VMEM_LIMIT = 56 * 1024 * 1024


def _cparams(sem, vmem=V7X_VMEM_LIMIT):
    return pltpu.CompilerParams(dimension_semantics=sem, vmem_limit_bytes=vmem)


def _dot_nt(a, b):
    return lax.dot_general(a, b, (((1,), (1,)), ((), ())), preferred_element_type=F32)


def _rmsnorm_body(x_ref, w_ref, o_ref):
    x = x_ref[...]
    ms = jnp.mean(x * x, axis=-1, keepdims=True)
    o_ref[...] = (x * lax.rsqrt(ms + EPS) * w_ref[...]).astype(o_ref.dtype)


def _rmsnorm(x2d, w, out_dtype):
    t, d = x2d.shape
    tm = min(256, t)
    return pl.pallas_call(
        _rmsnorm_body,
        out_shape=jax.ShapeDtypeStruct((t, d), out_dtype),
        grid=(t // tm,),
        in_specs=[pl.BlockSpec((tm, d), lambda i: (i, 0)), pl.BlockSpec((1, d), lambda i: (0, 0))],
        out_specs=pl.BlockSpec((tm, d), lambda i: (i, 0)),
        compiler_params=_cparams(("parallel",)),
        name="rmsnorm",
    )(x2d, w.reshape(1, d))


def _inproj_body(h_ref, wt_ref, o_ref):
    acc = _dot_nt(h_ref[...], wt_ref[...].astype(BF16))
    for c in range(o_ref.shape[0]):
        o_ref[c] = acc[:, c * HEAD_DIM:(c + 1) * HEAD_DIM].astype(o_ref.dtype)


def _inproj(h, w_in_t):
    t, d = h.shape
    tm = min(1024, t)
    tn = 512
    nb = tn // HEAD_DIM
    return pl.pallas_call(
        _inproj_body,
        out_shape=jax.ShapeDtypeStruct((N_MAIN_BLKS, t, HEAD_DIM), BF16),
        grid=(t // tm, MAIN_COLS // tn),
        in_specs=[pl.BlockSpec((tm, d), lambda i, j: (i, 0)), pl.BlockSpec((tn, d), lambda i, j: (j, 0))],
        out_specs=pl.BlockSpec((nb, tm, HEAD_DIM), lambda i, j: (j, i, 0)),
        compiler_params=_cparams(("parallel", "arbitrary")),
        name="inproj",
    )(h, w_in_t)


def _chunk_cumsum_rows(x):
    row = lax.broadcasted_iota(jnp.int32, x.shape, 0) % CHUNK
    shift = 1
    while shift < CHUNK:
        x = x + jnp.where(row >= shift, pltpu.roll(x, shift, 0), 0.0)
        shift *= 2
    return x


def _gates_body(h_ref, wt_ref, a_ref, dtb_ref, col_ref, row_ref):
    p = _dot_nt(h_ref[...], wt_ref[...].astype(BF16))
    lane = lax.broadcasted_iota(jnp.int32, p.shape, 1)
    beta = jax.nn.sigmoid(p)
    z = p + dtb_ref[...]
    softplus = jnp.maximum(z, 0.0) + jnp.log1p(jnp.exp(-jnp.abs(z)))
    g = -a_ref[...] * softplus
    out = jnp.where(lane < N_HEADS, beta, _chunk_cumsum_rows(g))
    col_ref[...] = out
    row_ref[...] = out.T[N_HEADS:2 * N_HEADS, :]


def _gates(h, w_small_t, a_log, dt_bias):
    t, d = h.shape
    tm = min(1024, t)
    pad = HEAD_DIM - 2 * N_HEADS
    w = jnp.pad(w_small_t, ((0, pad), (0, 0)))
    a = jnp.pad(jnp.exp(a_log.astype(F32)), (N_HEADS, pad)).reshape(1, HEAD_DIM)
    dtb = jnp.pad(dt_bias.astype(F32), (N_HEADS, pad)).reshape(1, HEAD_DIM)
    return pl.pallas_call(
        _gates_body,
        out_shape=(jax.ShapeDtypeStruct((t, HEAD_DIM), F32), jax.ShapeDtypeStruct((N_HEADS, t), F32)),
        grid=(t // tm,),
        in_specs=[
            pl.BlockSpec((tm, d), lambda i: (i, 0)),
            pl.BlockSpec((HEAD_DIM, d), lambda i: (0, 0)),
            pl.BlockSpec((1, HEAD_DIM), lambda i: (0, 0)),
            pl.BlockSpec((1, HEAD_DIM), lambda i: (0, 0)),
        ],
        out_specs=(pl.BlockSpec((tm, HEAD_DIM), lambda i: (i, 0)), pl.BlockSpec((N_HEADS, tm), lambda i: (0, i))),
        compiler_params=_cparams(("parallel",)),
        name="gates",
    )(h, w, a, dtb)


ATT_SUB = 128
ATT_WIN = ATT_SUB + LEFT_CHUNKS * CHUNK


def _attn_bias_tile(rel_bias):
    period = ATT_SUB + ATT_WIN
    offs = np.arange(period)
    key_minus_query = np.where(offs < ATT_WIN, offs, offs - period)
    rel = LEFT_CHUNKS * CHUNK - key_minus_query
    diag_vals = rel_bias[:, np.clip(rel, -REL_CLIP, REL_CLIP) + REL_CLIP].astype(F32)
    flat = jnp.tile(diag_vals, (1, ATT_SUB))[:, :ATT_SUB * (period - 1)]
    bias = flat.reshape(-1, ATT_SUB, period - 1)[:, :, :ATT_WIN]
    qc = np.arange(ATT_SUB)[:, None] // CHUNK
    kc = np.arange(ATT_WIN)[None, :] // CHUNK
    allowed = (kc >= qc) & (kc <= qc + LEFT_CHUNKS)
    return jnp.where(allowed[None], bias, NEG)


def _head_rmsnorm(x, w):
    xf = x.astype(F32)
    return xf * lax.rsqrt(jnp.mean(xf * xf, axis=-1, keepdims=True) + EPS) * w


def _attn_body(q_ref, kp_ref, kc_ref, vp_ref, vc_ref, bias_ref, qw_ref, kw_ref, o_ref):
    t = pl.program_id(1)
    q = _head_rmsnorm(q_ref[...], qw_ref[...]).astype(BF16)
    k = jnp.concatenate(
        [_head_rmsnorm(kp_ref[...], kw_ref[...]), _head_rmsnorm(kc_ref[...], kw_ref[...])], axis=0
    ).astype(BF16)
    v = jnp.concatenate([vp_ref[...], vc_ref[...]], axis=0)
    n_sub = SEQ_TILE // ATT_SUB
    qb = q.reshape(n_sub, ATT_SUB, HEAD_DIM)
    kb = jnp.stack([k[p * ATT_SUB:p * ATT_SUB + ATT_WIN] for p in range(n_sub)], axis=0)
    vb = jnp.stack([v[p * ATT_SUB:p * ATT_SUB + ATT_WIN] for p in range(n_sub)], axis=0)
    s = jnp.einsum("pqd,pkd->pqk", qb, kb, preferred_element_type=F32) + bias_ref[...][None]
    col = lax.broadcasted_iota(jnp.int32, (n_sub, ATT_SUB, ATT_WIN), 2)
    sub = lax.broadcasted_iota(jnp.int32, (n_sub, ATT_SUB, ATT_WIN), 0)
    first_valid = jnp.where(t > 0, 0, SEQ_TILE) - sub * jnp.where(t > 0, 0, ATT_SUB)
    s = jnp.where(col >= first_valid, s, NEG)
    m = jnp.max(s, axis=-1, keepdims=True)
    e = jnp.exp(s - m)
    l = jnp.sum(e, axis=-1, keepdims=True)
    o = jnp.einsum("pqk,pkd->pqd", e.astype(BF16), vb, preferred_element_type=F32) / l
    o_ref[...] = o.reshape(SEQ_TILE, HEAD_DIM).astype(o_ref.dtype)


def _band_attention(p3, bias_tile, q_norm_w, k_norm_w, bsz, seq):
    nt = seq // SEQ_TILE
    qw = (q_norm_w.astype(F32) * (HEAD_DIM ** -0.5)).reshape(1, HEAD_DIM)
    kw = k_norm_w.astype(F32).reshape(1, HEAD_DIM)

    def cur(blk):
        return lambda bh, t: (blk + bh % N_HEADS, (bh // N_HEADS) * nt + t, 0)

    def prev(blk):
        return lambda bh, t: (blk + bh % N_HEADS, (bh // N_HEADS) * nt + jnp.maximum(t - 1, 0), 0)

    tile = (None, SEQ_TILE, HEAD_DIM)
    return pl.pallas_call(
        _attn_body,
        out_shape=jax.ShapeDtypeStruct((bsz * seq, WIDTH), BF16),
        grid=(bsz * N_HEADS, nt),
        in_specs=[
            pl.BlockSpec(tile, cur(BLK_ATT_Q)),
            pl.BlockSpec(tile, prev(BLK_ATT_K)),
            pl.BlockSpec(tile, cur(BLK_ATT_K)),
            pl.BlockSpec(tile, prev(BLK_ATT_V)),
            pl.BlockSpec(tile, cur(BLK_ATT_V)),
            pl.BlockSpec((None, ATT_SUB, ATT_WIN), lambda bh, t: (bh % N_HEADS, 0, 0)),
            pl.BlockSpec((1, HEAD_DIM), lambda bh, t: (0, 0)),
            pl.BlockSpec((1, HEAD_DIM), lambda bh, t: (0, 0)),
        ],
        out_specs=pl.BlockSpec((SEQ_TILE, HEAD_DIM), lambda bh, t: ((bh // N_HEADS) * nt + t, bh % N_HEADS)),
        compiler_params=_cparams(("parallel", "arbitrary")),
        name="band_attention",
    )(p3, p3, p3, p3, p3, bias_tile, qw, kw)


CONV_HALO = 16


def _short_conv_silu(cur, halo, w, t):
    cur = cur.astype(F32)
    halo = jnp.where(t > 0, halo.astype(F32), 0.0)
    cat = jnp.concatenate([halo, cur], axis=0)
    y = cur * w[CONV_K - 1:CONV_K]
    for i in range(CONV_K - 1):
        shift = CONV_K - 1 - i
        y = y + pltpu.roll(cat, shift, 0)[CONV_HALO:] * w[i:i + 1]
    return y * jax.nn.sigmoid(y)


def _l2norm(x):
    return x * lax.rsqrt(jnp.sum(x * x, axis=-1, keepdims=True) + EPS)


def _bdot(a, b):
    return jnp.dot(a.astype(BF16), b.astype(BF16), preferred_element_type=F32)


def _bdot_nt(a, b):
    return lax.dot_general(a.astype(BF16), b.astype(BF16), (((1,), (1,)), ((), ())), preferred_element_type=F32)


GDN_HB = 8


def _bmm(a, b):
    return jnp.einsum("gmk,gkn->gmn", a.astype(BF16), b.astype(BF16), preferred_element_type=F32)


def _bmm_nt(a, b):
    return jnp.einsum("gmk,gnk->gmn", a.astype(BF16), b.astype(BF16), preferred_element_type=F32)


def _gdn_chunk_terms(q, k, v, beta, gc, grow):
    ri = lax.broadcasted_iota(jnp.int32, (1, CHUNK, CHUNK), 1)
    ci = lax.broadcasted_iota(jnp.int32, (1, CHUNK, CHUNK), 2)
    gamma = jnp.exp(jnp.where(ri >= ci, gc - grow, NEG))
    kb = k * beta
    kqk = _bmm_nt(jnp.concatenate([kb, q], axis=1), k)
    m = jnp.where(ri > ci, kqk[:, :CHUNK] * gamma, 0.0)
    att = kqk[:, CHUNK:] * gamma
    inv = (ri == ci).astype(F32) - m
    pw = _bmm(m, m)
    for _ in range(4):
        both = _bmm(jnp.concatenate([inv, pw], axis=1), pw)
        inv = inv + both[:, :CHUNK]
        pw = both[:, CHUNK:]
    inv = inv + _bmm(inv, pw)
    eg = jnp.exp(gc)
    g_last = gc[:, CHUNK - 1:CHUNK]
    sol = _bmm(inv, jnp.concatenate([v * beta, kb * eg], axis=2))
    att_uw = _bmm(att, sol)
    k_dec = k * jnp.exp(g_last - gc)
    kd_uw = _bmm(jnp.swapaxes(k_dec, 1, 2), sol)
    lhs = jnp.concatenate([q * eg - att_uw[:, :, HEAD_DIM:], kd_uw[:, :, HEAD_DIM:]], axis=1)
    return lhs, att_uw[:, :, :HEAD_DIM], kd_uw[:, :, :HEAD_DIM], jnp.exp(g_last)


def _gdn_body(q_ref, k_ref, v_ref, qh_ref, kh_ref, vh_ref, wq_ref, wk_ref, wv_ref, z_ref,
              gcol_ref, grow_ref, ow_ref, o_ref, state_ref):
    group = pl.program_id(0) % (N_HEADS // GDN_HB)
    t = pl.program_id(1)

    @pl.when(t == 0)
    def _():
        state_ref[...] = jnp.zeros_like(state_ref)

    gates = gcol_ref[...]
    lane = lax.broadcasted_iota(jnp.int32, gates.shape, 1)
    n_chunks = SEQ_TILE // CHUNK
    n_prob = GDN_HB * n_chunks

    qs, ks, vs, betas, gcs, grows = [], [], [], [], [], []
    for hh in range(GDN_HB):
        head = group * GDN_HB + hh
        qs.append(_l2norm(_short_conv_silu(q_ref[hh], qh_ref[hh], wq_ref[hh], t)) * (HEAD_DIM ** -0.5))
        ks.append(_l2norm(_short_conv_silu(k_ref[hh], kh_ref[hh], wk_ref[hh], t)))
        vs.append(_short_conv_silu(v_ref[hh], vh_ref[hh], wv_ref[hh], t))
        betas.append(jnp.sum(jnp.where(lane == head, gates, 0.0), axis=-1, keepdims=True))
        gcs.append(jnp.sum(jnp.where(lane == head + N_HEADS, gates, 0.0), axis=-1, keepdims=True))
        grow = grow_ref[hh]
        grows.extend(grow[:, c * CHUNK:(c + 1) * CHUNK] for c in range(n_chunks))

    def chunked(parts):
        return jnp.stack(parts, axis=0).reshape(n_prob, CHUNK, parts[0].shape[-1])

    lhs, o_intra, s_add, decay = _gdn_chunk_terms(
        chunked(qs), chunked(ks), chunked(vs), chunked(betas), chunked(gcs), jnp.stack(grows, axis=0))
    lhs = lhs.reshape(GDN_HB, n_chunks, CHUNK + HEAD_DIM, HEAD_DIM)
    o_intra = o_intra.reshape(GDN_HB, n_chunks, CHUNK, HEAD_DIM)
    s_add = s_add.reshape(GDN_HB, n_chunks, HEAD_DIM, HEAD_DIM)
    decay = decay.reshape(GDN_HB, n_chunks, 1, 1)

    state = state_ref[...]
    ow = ow_ref[...]
    for c in range(n_chunks):
        prod = _bmm(lhs[:, c], state)
        o_c = prod[:, :CHUNK] + o_intra[:, c]
        state = state * decay[:, c] - prod[:, CHUNK:] + s_add[:, c]
        o_n = o_c * lax.rsqrt(jnp.mean(o_c * o_c, axis=-1, keepdims=True) + EPS) * ow
        for hh in range(GDN_HB):
            zc = z_ref[hh, c * CHUNK:(c + 1) * CHUNK, :].astype(F32)
            o_ref[c * CHUNK:(c + 1) * CHUNK, hh * HEAD_DIM:(hh + 1) * HEAD_DIM] = (
                o_n[hh] * (zc * jax.nn.sigmoid(zc))).astype(o_ref.dtype)
    state_ref[...] = state


def _gated_delta(p3, conv_w, gcol, grow, o_norm_w, bsz, seq):
    nt = seq // SEQ_TILE
    halos_per_tile = SEQ_TILE // CONV_HALO
    n_groups = N_HEADS // GDN_HB
    cw = conv_w.astype(F32).reshape(CONV_K, 3 * N_HEADS, HEAD_DIM).transpose(1, 0, 2)
    grow3 = grow.reshape(N_HEADS, 1, bsz * seq)

    def cur(blk):
        return lambda bg, t: (blk // GDN_HB + bg % n_groups, (bg // n_groups) * nt + t, 0)

    def halo(blk):
        return lambda bg, t: (
            blk // GDN_HB + bg % n_groups, jnp.maximum(((bg // n_groups) * nt + t) * halos_per_tile - 1, 0), 0)

    def cwmap(off):
        return lambda bg, t: (off // GDN_HB + bg % n_groups, 0, 0)

    tile = (GDN_HB, SEQ_TILE, HEAD_DIM)
    htile = (GDN_HB, CONV_HALO, HEAD_DIM)
    wtile = (GDN_HB, CONV_K, HEAD_DIM)
    return pl.pallas_call(
        _gdn_body,
        out_shape=jax.ShapeDtypeStruct((bsz * seq, WIDTH), BF16),
        grid=(bsz * n_groups, nt),
        in_specs=[
            pl.BlockSpec(tile, cur(BLK_DN_Q)),
            pl.BlockSpec(tile, cur(BLK_DN_K)),
            pl.BlockSpec(tile, cur(BLK_DN_V)),
            pl.BlockSpec(htile, halo(BLK_DN_Q)),
            pl.BlockSpec(htile, halo(BLK_DN_K)),
            pl.BlockSpec(htile, halo(BLK_DN_V)),
            pl.BlockSpec(wtile, cwmap(0)),
            pl.BlockSpec(wtile, cwmap(N_HEADS)),
            pl.BlockSpec(wtile, cwmap(2 * N_HEADS)),
            pl.BlockSpec(tile, cur(BLK_DN_Z)),
            pl.BlockSpec((SEQ_TILE, HEAD_DIM), lambda bg, t: ((bg // n_groups) * nt + t, 0)),
            pl.BlockSpec((GDN_HB, 1, SEQ_TILE), lambda bg, t: (bg % n_groups, 0, (bg // n_groups) * nt + t)),
            pl.BlockSpec((1, HEAD_DIM), lambda bg, t: (0, 0)),
        ],
        out_specs=pl.BlockSpec(
            (SEQ_TILE, GDN_HB * HEAD_DIM), lambda bg, t: ((bg // n_groups) * nt + t, bg % n_groups)),
        scratch_shapes=[pltpu.VMEM((GDN_HB, HEAD_DIM, HEAD_DIM), F32)],
        compiler_params=_cparams(("parallel", "arbitrary")),
        name="gated_delta",
    )(p3, p3, p3, p3, p3, p3, cw, cw, cw, p3, gcol, grow3, o_norm_w.astype(F32).reshape(1, HEAD_DIM))


def _outproj_body(ya_ref, yb_ref, wa_ref, wb_ref, x_ref, o_ref):
    acc = jnp.dot(ya_ref[...], wa_ref[...].astype(BF16), preferred_element_type=F32)
    acc = acc + jnp.dot(yb_ref[...], wb_ref[...].astype(BF16), preferred_element_type=F32)
    o_ref[...] = x_ref[...] + acc


def _outproj(ya, yb, w_out, x2d):
    t, d = x2d.shape
    tm = min(1024, t)
    tn = min(512, d)
    return pl.pallas_call(
        _outproj_body,
        out_shape=jax.ShapeDtypeStruct((t, d), F32),
        grid=(t // tm, d // tn),
        in_specs=[
            pl.BlockSpec((tm, WIDTH), lambda i, j: (i, 0)),
            pl.BlockSpec((tm, WIDTH), lambda i, j: (i, 0)),
            pl.BlockSpec((WIDTH, tn), lambda i, j: (0, j)),
            pl.BlockSpec((WIDTH, tn), lambda i, j: (1, j)),
            pl.BlockSpec((tm, tn), lambda i, j: (i, j)),
        ],
        out_specs=pl.BlockSpec((tm, tn), lambda i, j: (i, j)),
        compiler_params=_cparams(("parallel", "arbitrary")),
        name="outproj",
    )(ya, yb, w_out, w_out, x2d)


def _router_body(x_ref, nw_ref, wr_ref, h_ref, r_ref):
    x = x_ref[...]
    h = x * lax.rsqrt(jnp.mean(x * x, axis=-1, keepdims=True) + EPS) * nw_ref[...]
    h_ref[...] = h
    logits = jnp.dot(h, wr_ref[...], precision=lax.Precision.HIGHEST, preferred_element_type=F32)
    lane = lax.broadcasted_iota(jnp.int32, logits.shape, 1).astype(F32)
    big = float(HEAD_DIM)

    def first_argmax(vals, vmax):
        return jnp.min(jnp.where(vals == vmax, lane, big), axis=-1, keepdims=True)

    gl = jnp.where(lane < N_GROUPS, logits, NEG)
    gmax = jnp.max(gl, axis=-1, keepdims=True)
    gidx = first_argmax(gl, gmax)
    grp_w = 1.0 / jnp.sum(jnp.exp(gl - gmax), axis=-1, keepdims=True)
    lo = N_GROUPS + gidx * EXPERTS_PER_GROUP
    el = jnp.where((lane >= lo) & (lane < lo + EXPERTS_PER_GROUP), logits, NEG)
    m1 = jnp.max(el, axis=-1, keepdims=True)
    i1 = first_argmax(el, m1)
    el2 = jnp.where(lane == i1, NEG, el)
    m2 = jnp.max(el2, axis=-1, keepdims=True)
    i2 = first_argmax(el2, m2)
    r = jnp.exp(m2 - m1)
    w1 = 1.0 / (1.0 + r)
    w2 = r / (1.0 + r)
    out = jnp.where(lane == 0, i1 - N_GROUPS, 0.0)
    out = jnp.where(lane == 1, i2 - N_GROUPS, out)
    out = jnp.where(lane == 2, grp_w * w1, out)
    out = jnp.where(lane == 3, grp_w * w2, out)
    r_ref[...] = out


def _norm_router(x1, norm_w, w_group, w_router):
    t, d = x1.shape
    tm = min(256, t)
    wr = jnp.concatenate([w_group, w_router], axis=1).astype(F32)
    wr = jnp.pad(wr, ((0, 0), (0, HEAD_DIM - wr.shape[1])))
    return pl.pallas_call(
        _router_body,
        out_shape=(jax.ShapeDtypeStruct((t, d), F32), jax.ShapeDtypeStruct((t, HEAD_DIM), F32)),
        grid=(t // tm,),
        in_specs=[
            pl.BlockSpec((tm, d), lambda i: (i, 0)),
            pl.BlockSpec((1, d), lambda i: (0, 0)),
            pl.BlockSpec((d, HEAD_DIM), lambda i: (0, 0)),
        ],
        out_specs=(pl.BlockSpec((tm, d), lambda i: (i, 0)), pl.BlockSpec((tm, HEAD_DIM), lambda i: (i, 0))),
        compiler_params=_cparams(("parallel",)),
        name="norm_router",
    )(x1, norm_w.astype(F32).reshape(1, d), wr)


def _row_gather_start(src_hbm, idx_ref, dst, sem, row0, n_rows):
    def body(r, carry):
        rr = row0 + r
        pltpu.make_async_copy(src_hbm.at[pl.ds(idx_ref[0, rr], 1)], dst.at[pl.ds(rr, 1)], sem).start()
        return carry

    lax.fori_loop(0, n_rows, body, 0, unroll=8)


def _row_gather_wait(dst, sem):
    pltpu.make_async_copy(dst, dst, sem).wait()


FFN_SUB = 256
FFN_RMAX = 3 * FFN_SUB
FFN_KSTEPS = 4
FFN_NSTEPS = 4
FFN_STEPS = FFN_KSTEPS + FFN_NSTEPS
FFN_GATHER_ROWS = FFN_RMAX // FFN_STEPS


def _ffn_body(nused_ref, vise_ref, nsub_ref, tok_ref, tokn_ref, x_hbm, wg_ref, wu_ref, wd_ref, y_ref,
              stage, xb, acc, hid, sem):
    v = pl.program_id(0)
    s = pl.program_id(1)
    nused = nused_ref[0]
    nsub = nsub_ref[v]
    kc = wg_ref.shape[0]

    @pl.when((v == 0) & (s == 0) & (nused > 0))
    def _():
        _row_gather_start(x_hbm, tok_ref, stage, sem, 0, FFN_RMAX)

    @pl.when((s == 0) & (v < nused))
    def _():
        _row_gather_wait(stage, sem)
        xb[...] = stage[...].astype(BF16)

    for nb in range(1, FFN_RMAX // FFN_SUB + 1):
        m = nb * FFN_SUB

        @pl.when((nsub == nb) & (s < FFN_KSTEPS))
        def _(m=m):
            xk = xb[0:m, pl.ds(pl.multiple_of(s * kc, kc), kc)]
            g = jnp.dot(xk, wg_ref[...].astype(BF16), preferred_element_type=F32)
            u = jnp.dot(xk, wu_ref[...].astype(BF16), preferred_element_type=F32)

            @pl.when(s == 0)
            def _():
                acc[0:m, 0:D_EXPERT] = g
                acc[0:m, D_EXPERT:] = u

            @pl.when((s > 0) & (s < FFN_KSTEPS - 1))
            def _():
                acc[0:m, 0:D_EXPERT] += g
                acc[0:m, D_EXPERT:] += u

            @pl.when(s == FFN_KSTEPS - 1)
            def _():
                gg = acc[0:m, 0:D_EXPERT] + g
                uu = acc[0:m, D_EXPERT:] + u
                hid[0:m, :] = (gg * jax.nn.sigmoid(gg) * uu).astype(BF16)

        @pl.when((nsub == nb) & (s >= FFN_KSTEPS))
        def _(m=m):
            y_ref[0:m, :] = jnp.dot(hid[0:m, :], wd_ref[...].astype(BF16), preferred_element_type=F32)
            if m < FFN_RMAX:
                y_ref[m:, :] = jnp.zeros((FFN_RMAX - m, y_ref.shape[1]), F32)

    @pl.when(nsub == 0)
    def _():
        y_ref[...] = jnp.zeros_like(y_ref)

    @pl.when(v + 1 < nused)
    def _():
        _row_gather_start(x_hbm, tokn_ref, stage, sem, s * FFN_GATHER_ROWS, FFN_GATHER_ROWS)


def _expert_ffn(h2, w_gate, w_up, w_down, nused, vis_e, vis_nsub, slot_tok):
    t, d = h2.shape
    n_vis = vis_e.shape[0]
    assert d % (FFN_KSTEPS * HEAD_DIM) == 0 and d % (FFN_NSTEPS * HEAD_DIM) == 0
    tok_arr = slot_tok.reshape(n_vis, 1, FFN_RMAX)
    smem_blk = functools.partial(pl.BlockSpec, (None, 1, FFN_RMAX), memory_space=pltpu.SMEM)
    kstep = lambda s: jnp.minimum(s, FFN_KSTEPS - 1)
    nstep = lambda s: jnp.clip(s - FFN_KSTEPS, 0, FFN_NSTEPS - 1)
    grid_spec = pltpu.PrefetchScalarGridSpec(
        num_scalar_prefetch=3,
        grid=(n_vis, FFN_STEPS),
        in_specs=[
            smem_blk(lambda v, s, nu, ve, ns: (v, 0, 0)),
            smem_blk(lambda v, s, nu, ve, ns: (jnp.minimum(v + 1, n_vis - 1), 0, 0)),
            pl.BlockSpec(memory_space=pl.ANY),
            pl.BlockSpec((None, d // FFN_KSTEPS, D_EXPERT), lambda v, s, nu, ve, ns: (ve[v], kstep(s), 0)),
            pl.BlockSpec((None, d // FFN_KSTEPS, D_EXPERT), lambda v, s, nu, ve, ns: (ve[v], kstep(s), 0)),
            pl.BlockSpec((None, D_EXPERT, d // FFN_NSTEPS), lambda v, s, nu, ve, ns: (ve[v], 0, nstep(s))),
        ],
        out_specs=pl.BlockSpec((FFN_RMAX, d // FFN_NSTEPS), lambda v, s, nu, ve, ns: (v, nstep(s))),
        scratch_shapes=[
            pltpu.VMEM((FFN_RMAX, d), F32),
            pltpu.VMEM((FFN_RMAX, d), BF16),
            pltpu.VMEM((FFN_RMAX, 2 * D_EXPERT), F32),
            pltpu.VMEM((FFN_RMAX, D_EXPERT), BF16),
            pltpu.SemaphoreType.DMA(()),
        ],
    )
    return pl.pallas_call(
        _ffn_body,
        out_shape=jax.ShapeDtypeStruct((n_vis * FFN_RMAX, d), F32),
        grid_spec=grid_spec,
        compiler_params=_cparams(("arbitrary", "arbitrary")),
        name="expert_ffn",
    )(nused, vis_e, vis_nsub, tok_arr, tok_arr, h2, w_gate, w_up, w_down)


def _combine_body(d0_ref, d1_ref, d0n_ref, d1n_ref, x_ref, r_ref, y_hbm, o_ref, ybuf, sem):
    i = pl.program_id(0)
    n = pl.num_programs(0)
    slot = i % 2

    @pl.when(i == 0)
    def _():
        _row_gather_start(y_hbm, d0_ref, ybuf.at[0, 0], sem.at[0], 0, COMBINE_TM)
        _row_gather_start(y_hbm, d1_ref, ybuf.at[0, 1], sem.at[0], 0, COMBINE_TM)

    @pl.when(i + 1 < n)
    def _():
        _row_gather_start(y_hbm, d0n_ref, ybuf.at[1 - slot, 0], sem.at[1 - slot], 0, COMBINE_TM)
        _row_gather_start(y_hbm, d1n_ref, ybuf.at[1 - slot, 1], sem.at[1 - slot], 0, COMBINE_TM)

    _row_gather_wait(ybuf.at[slot], sem.at[slot])
    route = r_ref[...]
    o_ref[...] = x_ref[...] + route[:, 2:3] * ybuf[slot, 0] + route[:, 3:4] * ybuf[slot, 1]


def _combine(x1, route, y, dest0, dest1):
    t, d = x1.shape
    tm = min(COMBINE_TM, t)
    assert tm == COMBINE_TM
    n = t // tm
    d0 = dest0.reshape(n, 1, tm)
    d1 = dest1.reshape(n, 1, tm)
    smem_blk = functools.partial(pl.BlockSpec, (None, 1, tm), memory_space=pltpu.SMEM)
    nxt = lambda i: (jnp.minimum(i + 1, n - 1), 0, 0)
    return pl.pallas_call(
        _combine_body,
        out_shape=jax.ShapeDtypeStruct((t, d), F32),
        grid=(n,),
        in_specs=[
            smem_blk(lambda i: (i, 0, 0)),
            smem_blk(lambda i: (i, 0, 0)),
            smem_blk(nxt),
            smem_blk(nxt),
            pl.BlockSpec((tm, d), lambda i: (i, 0)),
            pl.BlockSpec((tm, HEAD_DIM), lambda i: (i, 0)),
            pl.BlockSpec(memory_space=pl.ANY),
        ],
        out_specs=pl.BlockSpec((tm, d), lambda i: (i, 0)),
        scratch_shapes=[pltpu.VMEM((2, 2, tm, d), F32), pltpu.SemaphoreType.DMA((2,))],
        compiler_params=_cparams(("arbitrary",)),
        name="moe_combine",
    )(d0, d1, d0, d1, x1, route, y)


def _dispatch_plan(route, t):
    expert = route[:, 0:2].astype(jnp.int32)
    n_assign = 2 * t
    flat_e = expert.reshape(n_assign)
    order = jnp.argsort(flat_e).astype(jnp.int32)
    e_s = flat_e[order]
    below = jnp.sum(flat_e[None, :] < jnp.arange(N_EXPERTS + 1, dtype=jnp.int32)[:, None], axis=1)
    start = below[:-1].astype(jnp.int32)
    counts = (below[1:] - below[:-1]).astype(jnp.int32)
    n_vis_e = (counts + FFN_RMAX - 1) // FFN_RMAX
    vis_end = jnp.cumsum(n_vis_e)
    vis_start = vis_end - n_vis_e
    dest_sorted = vis_start[e_s] * FFN_RMAX + (jnp.arange(n_assign, dtype=jnp.int32) - start[e_s])
    _, dest = lax.sort_key_val(order, dest_sorted)
    dest = dest.reshape(t, 2)
    n_vis = -(-n_assign // FFN_RMAX) + N_EXPERTS
    nused = vis_end[-1].astype(jnp.int32).reshape(1)
    vis = jnp.arange(n_vis, dtype=jnp.int32)
    vis_e = jnp.minimum(jnp.sum(vis_end[None, :] <= vis[:, None], axis=1), N_EXPERTS - 1).astype(jnp.int32)
    vis_off = (vis - vis_start[vis_e]) * FFN_RMAX
    vis_rows = jnp.where(vis < nused[0], jnp.clip(counts[vis_e] - vis_off, 0, FFN_RMAX), 0)
    vis_nsub = ((vis_rows + FFN_SUB - 1) // FFN_SUB).astype(jnp.int32)
    row = jnp.arange(FFN_RMAX, dtype=jnp.int32)[None, :]
    src = jnp.clip((start[vis_e] + vis_off)[:, None] + row, 0, n_assign - 1).reshape(-1)
    valid = (row < vis_rows[:, None]).reshape(-1)
    slot_tok = jnp.where(valid, order[src] // 2, 0).astype(jnp.int32)
    last_e = vis_e[jnp.maximum(nused[0] - 1, 0)]
    vis_e = jnp.where(vis < nused[0], vis_e, last_e)
    return nused, vis_e, vis_nsub, slot_tok, dest[:, 0], dest[:, 1]


def _layer(x, norm1_w, w_in, q_norm_w, k_norm_w, rel_bias, conv_w, a_log, dt_bias,
           o_norm_w, w_out, norm2_w, w_group, w_router, w_gate, w_up, w_down):
    bsz, seq, d = x.shape
    t = bsz * seq
    assert seq % SEQ_TILE == 0 and d % HEAD_DIM == 0 and t % COMBINE_TM == 0
    assert w_in.shape[1] == MAIN_COLS + 2 * N_HEADS
    x2d = x.reshape(t, d)

    w_in_t = jnp.swapaxes(w_in, 0, 1)
    h1 = _rmsnorm(x2d, norm1_w.astype(F32), BF16)
    p3 = _inproj(h1, w_in_t)
    gcol, grow = _gates(h1, w_in_t[MAIN_COLS:], a_log, dt_bias)

    ya = _band_attention(p3, _attn_bias_tile(rel_bias), q_norm_w, k_norm_w, bsz, seq)
    yb = _gated_delta(p3, conv_w, gcol, grow, o_norm_w, bsz, seq)
    x1 = _outproj(ya, yb, w_out, x2d)

    h2, route = _norm_router(x1, norm2_w, w_group, w_router)
    nused, vis_e, vis_nsub, slot_tok, dest0, dest1 = _dispatch_plan(route, t)
    y = _expert_ffn(h2, w_gate, w_up, w_down, nused, vis_e, vis_nsub, slot_tok)
    out = _combine(x1, route, y, dest0, dest1)
    return out.reshape(bsz, seq, d)


def kernel(x, norm1_w, w_in, q_norm_w, k_norm_w, rel_bias, conv_w, a_log, dt_bias, o_norm_w, w_out,
           norm2_w, w_group, w_router, w_gate, w_up, w_down):
    for l in range(norm1_w.shape[0]):
        x = _layer(x, norm1_w[l], w_in[l], q_norm_w[l], k_norm_w[l], rel_bias[l], conv_w[l], a_log[l],
                   dt_bias[l], o_norm_w[l], w_out[l], norm2_w[l], w_group[l], w_router[l], w_gate[l],
                   w_up[l], w_down[l])
    return x
```

```python
import functools

import jax
import jax.numpy as jnp
import numpy as np
from jax import lax
from jax.experimental import pallas as pl
from jax.experimental.pallas import tpu as pltpu

F32 = jnp.float32
BF16 = jnp.bfloat16

CHUNK = 64
HEAD_DIM = 128
N_HEADS = 16
WIDTH = N_HEADS * HEAD_DIM
LEFT_CHUNKS = 8
REL_CLIP = 256
CONV_K = 4
N_GROUPS = 8
EXPERTS_PER_GROUP = 8
N_EXPERTS = N_GROUPS * EXPERTS_PER_GROUP
D_EXPERT = 512
EPS = 1e-6
NEG = -1e30

BLK_ATT_Q, BLK_ATT_K, BLK_ATT_V = 0, 16, 32
BLK_DN_Q, BLK_DN_K, BLK_DN_V, BLK_DN_Z = 48, 64, 80, 96
N_MAIN_BLKS = 112
MAIN_COLS = N_MAIN_BLKS * HEAD_DIM

SEQ_TILE = 512
COMBINE_TM = 256

V7X_VMEM_LIMIT = 56 * 1024 * 1024


def _cparams(sem, vmem=V7X_VMEM_LIMIT):
    return pltpu.CompilerParams(dimension_semantics=sem, vmem_limit_bytes=vmem)


def _dot_nt(a, b):
    return lax.dot_general(a, b, (((1,), (1,)), ((), ())), preferred_element_type=F32)


def _rmsnorm_body(x_ref, w_ref, o_ref):
    x = x_ref[...]
    ms = jnp.mean(x * x, axis=-1, keepdims=True)
    o_ref[...] = (x * lax.rsqrt(ms + EPS) * w_ref[...]).astype(o_ref.dtype)


def _rmsnorm(x2d, w, out_dtype):
    t, d = x2d.shape
    tm = min(256, t)
    return pl.pallas_call(
        _rmsnorm_body,
        out_shape=jax.ShapeDtypeStruct((t, d), out_dtype),
        grid=(t // tm,),
        in_specs=[pl.BlockSpec((tm, d), lambda i: (i, 0)), pl.BlockSpec((1, d), lambda i: (0, 0))],
        out_specs=pl.BlockSpec((tm, d), lambda i: (i, 0)),
        compiler_params=_cparams(("parallel",)),
        name="rmsnorm",
    )(x2d, w.reshape(1, d))


def _inproj_body(h_ref, wt_ref, o_ref):
    acc = _dot_nt(h_ref[...], wt_ref[...].astype(BF16))
    for c in range(o_ref.shape[0]):
        o_ref[c] = acc[:, c * HEAD_DIM:(c + 1) * HEAD_DIM].astype(o_ref.dtype)


def _inproj(h, w_in_t):
    t, d = h.shape
    tm = min(1024, t)
    tn = 512
    nb = tn // HEAD_DIM
    return pl.pallas_call(
        _inproj_body,
        out_shape=jax.ShapeDtypeStruct((N_MAIN_BLKS, t, HEAD_DIM), BF16),
        grid=(t // tm, MAIN_COLS // tn),
        in_specs=[pl.BlockSpec((tm, d), lambda i, j: (i, 0)), pl.BlockSpec((tn, d), lambda i, j: (j, 0))],
        out_specs=pl.BlockSpec((nb, tm, HEAD_DIM), lambda i, j: (j, i, 0)),
        compiler_params=_cparams(("parallel", "arbitrary")),
        name="inproj",
    )(h, w_in_t)


def _chunk_cumsum_rows(x):
    row = lax.broadcasted_iota(jnp.int32, x.shape, 0) % CHUNK
    shift = 1
    while shift < CHUNK:
        x = x + jnp.where(row >= shift, pltpu.roll(x, shift, 0), 0.0)
        shift *= 2
    return x


def _gates_body(h_ref, wt_ref, a_ref, dtb_ref, col_ref, row_ref):
    p = _dot_nt(h_ref[...], wt_ref[...].astype(BF16))
    lane = lax.broadcasted_iota(jnp.int32, p.shape, 1)
    beta = jax.nn.sigmoid(p)
    z = p + dtb_ref[...]
    softplus = jnp.maximum(z, 0.0) + jnp.log1p(jnp.exp(-jnp.abs(z)))
    g = -a_ref[...] * softplus
    out = jnp.where(lane < N_HEADS, beta, _chunk_cumsum_rows(g))
    col_ref[...] = out
    row_ref[...] = out.T[N_HEADS:2 * N_HEADS, :]


def _gates(h, w_small_t, a_log, dt_bias):
    t, d = h.shape
    tm = min(1024, t)
    pad = HEAD_DIM - 2 * N_HEADS
    w = jnp.pad(w_small_t, ((0, pad), (0, 0)))
    a = jnp.pad(jnp.exp(a_log.astype(F32)), (N_HEADS, pad)).reshape(1, HEAD_DIM)
    dtb = jnp.pad(dt_bias.astype(F32), (N_HEADS, pad)).reshape(1, HEAD_DIM)
    return pl.pallas_call(
        _gates_body,
        out_shape=(jax.ShapeDtypeStruct((t, HEAD_DIM), F32), jax.ShapeDtypeStruct((N_HEADS, t), F32)),
        grid=(t // tm,),
        in_specs=[
            pl.BlockSpec((tm, d), lambda i: (i, 0)),
            pl.BlockSpec((HEAD_DIM, d), lambda i: (0, 0)),
            pl.BlockSpec((1, HEAD_DIM), lambda i: (0, 0)),
            pl.BlockSpec((1, HEAD_DIM), lambda i: (0, 0)),
        ],
        out_specs=(pl.BlockSpec((tm, HEAD_DIM), lambda i: (i, 0)), pl.BlockSpec((N_HEADS, tm), lambda i: (0, i))),
        compiler_params=_cparams(("parallel",)),
        name="gates",
    )(h, w, a, dtb)


ATT_SUB = 128
ATT_WIN = ATT_SUB + LEFT_CHUNKS * CHUNK


def _attn_bias_tile(rel_bias):
    period = ATT_SUB + ATT_WIN
    offs = np.arange(period)
    key_minus_query = np.where(offs < ATT_WIN, offs, offs - period)
    rel = LEFT_CHUNKS * CHUNK - key_minus_query
    diag_vals = rel_bias[:, np.clip(rel, -REL_CLIP, REL_CLIP) + REL_CLIP].astype(F32)
    flat = jnp.tile(diag_vals, (1, ATT_SUB))[:, :ATT_SUB * (period - 1)]
    bias = flat.reshape(-1, ATT_SUB, period - 1)[:, :, :ATT_WIN]
    qc = np.arange(ATT_SUB)[:, None] // CHUNK
    kc = np.arange(ATT_WIN)[None, :] // CHUNK
    allowed = (kc >= qc) & (kc <= qc + LEFT_CHUNKS)
    return jnp.where(allowed[None], bias, NEG)


def _head_rmsnorm(x, w):
    xf = x.astype(F32)
    return xf * lax.rsqrt(jnp.mean(xf * xf, axis=-1, keepdims=True) + EPS) * w


def _attn_body(q_ref, kp_ref, kc_ref, vp_ref, vc_ref, bias_ref, qw_ref, kw_ref, o_ref):
    t = pl.program_id(1)
    q = _head_rmsnorm(q_ref[...], qw_ref[...]).astype(BF16)
    k = jnp.concatenate(
        [_head_rmsnorm(kp_ref[...], kw_ref[...]), _head_rmsnorm(kc_ref[...], kw_ref[...])], axis=0
    ).astype(BF16)
    v = jnp.concatenate([vp_ref[...], vc_ref[...]], axis=0)
    n_sub = SEQ_TILE // ATT_SUB
    qb = q.reshape(n_sub, ATT_SUB, HEAD_DIM)
    kb = jnp.stack([k[p * ATT_SUB:p * ATT_SUB + ATT_WIN] for p in range(n_sub)], axis=0)
    vb = jnp.stack([v[p * ATT_SUB:p * ATT_SUB + ATT_WIN] for p in range(n_sub)], axis=0)
    s = jnp.einsum("pqd,pkd->pqk", qb, kb, preferred_element_type=F32) + bias_ref[...][None]
    col = lax.broadcasted_iota(jnp.int32, (n_sub, ATT_SUB, ATT_WIN), 2)
    sub = lax.broadcasted_iota(jnp.int32, (n_sub, ATT_SUB, ATT_WIN), 0)
    first_valid = jnp.where(t > 0, 0, SEQ_TILE) - sub * jnp.where(t > 0, 0, ATT_SUB)
    s = jnp.where(col >= first_valid, s, NEG)
    m = jnp.max(s, axis=-1, keepdims=True)
    e = jnp.exp(s - m)
    l = jnp.sum(e, axis=-1, keepdims=True)
    o = jnp.einsum("pqk,pkd->pqd", e.astype(BF16), vb, preferred_element_type=F32) / l
    o_ref[...] = o.reshape(SEQ_TILE, HEAD_DIM).astype(o_ref.dtype)


def _band_attention(p3, bias_tile, q_norm_w, k_norm_w, bsz, seq):
    nt = seq // SEQ_TILE
    qw = (q_norm_w.astype(F32) * (HEAD_DIM ** -0.5)).reshape(1, HEAD_DIM)
    kw = k_norm_w.astype(F32).reshape(1, HEAD_DIM)

    def cur(blk):
        return lambda bh, t: (blk + bh % N_HEADS, (bh // N_HEADS) * nt + t, 0)

    def prev(blk):
        return lambda bh, t: (blk + bh % N_HEADS, (bh // N_HEADS) * nt + jnp.maximum(t - 1, 0), 0)

    tile = (None, SEQ_TILE, HEAD_DIM)
    return pl.pallas_call(
        _attn_body,
        out_shape=jax.ShapeDtypeStruct((bsz * seq, WIDTH), BF16),
        grid=(bsz * N_HEADS, nt),
        in_specs=[
            pl.BlockSpec(tile, cur(BLK_ATT_Q)),
            pl.BlockSpec(tile, prev(BLK_ATT_K)),
            pl.BlockSpec(tile, cur(BLK_ATT_K)),
            pl.BlockSpec(tile, prev(BLK_ATT_V)),
            pl.BlockSpec(tile, cur(BLK_ATT_V)),
            pl.BlockSpec((None, ATT_SUB, ATT_WIN), lambda bh, t: (bh % N_HEADS, 0, 0)),
            pl.BlockSpec((1, HEAD_DIM), lambda bh, t: (0, 0)),
            pl.BlockSpec((1, HEAD_DIM), lambda bh, t: (0, 0)),
        ],
        out_specs=pl.BlockSpec((SEQ_TILE, HEAD_DIM), lambda bh, t: ((bh // N_HEADS) * nt + t, bh % N_HEADS)),
        compiler_params=_cparams(("parallel", "arbitrary")),
        name="band_attention",
    )(p3, p3, p3, p3, p3, bias_tile, qw, kw)


CONV_HALO = 16


def _short_conv_silu(cur, halo, w, t):
    cur = cur.astype(F32)
    halo = jnp.where(t > 0, halo.astype(F32), 0.0)
    cat = jnp.concatenate([halo, cur], axis=0)
    y = cur * w[CONV_K - 1:CONV_K]
    for i in range(CONV_K - 1):
        shift = CONV_K - 1 - i
        y = y + pltpu.roll(cat, shift, 0)[CONV_HALO:] * w[i:i + 1]
    return y * jax.nn.sigmoid(y)


def _l2norm(x):
    return x * lax.rsqrt(jnp.sum(x * x, axis=-1, keepdims=True) + EPS)


def _bdot(a, b):
    return jnp.dot(a.astype(BF16), b.astype(BF16), preferred_element_type=F32)


def _bdot_nt(a, b):
    return lax.dot_general(a.astype(BF16), b.astype(BF16), (((1,), (1,)), ((), ())), preferred_element_type=F32)


GDN_HB = 8


def _bmm(a, b):
    return jnp.einsum("gmk,gkn->gmn", a.astype(BF16), b.astype(BF16), preferred_element_type=F32)


def _bmm_nt(a, b):
    return jnp.einsum("gmk,gnk->gmn", a.astype(BF16), b.astype(BF16), preferred_element_type=F32)


def _gdn_chunk_terms(q, k, v, beta, gc, grow):
    ri = lax.broadcasted_iota(jnp.int32, (1, CHUNK, CHUNK), 1)
    ci = lax.broadcasted_iota(jnp.int32, (1, CHUNK, CHUNK), 2)
    gamma = jnp.exp(jnp.where(ri >= ci, gc - grow, NEG))
    kb = k * beta
    kqk = _bmm_nt(jnp.concatenate([kb, q], axis=1), k)
    m = jnp.where(ri > ci, kqk[:, :CHUNK] * gamma, 0.0)
    att = kqk[:, CHUNK:] * gamma
    inv = (ri == ci).astype(F32) - m
    pw = _bmm(m, m)
    for _ in range(4):
        both = _bmm(jnp.concatenate([inv, pw], axis=1), pw)
        inv = inv + both[:, :CHUNK]
        pw = both[:, CHUNK:]
    inv = inv + _bmm(inv, pw)
    eg = jnp.exp(gc)
    g_last = gc[:, CHUNK - 1:CHUNK]
    sol = _bmm(inv, jnp.concatenate([v * beta, kb * eg], axis=2))
    att_uw = _bmm(att, sol)
    k_dec = k * jnp.exp(g_last - gc)
    kd_uw = _bmm(jnp.swapaxes(k_dec, 1, 2), sol)
    lhs = jnp.concatenate([q * eg - att_uw[:, :, HEAD_DIM:], kd_uw[:, :, HEAD_DIM:]], axis=1)
    return lhs, att_uw[:, :, :HEAD_DIM], kd_uw[:, :, :HEAD_DIM], jnp.exp(g_last)


def _gdn_body(q_ref, k_ref, v_ref, qh_ref, kh_ref, vh_ref, wq_ref, wk_ref, wv_ref, z_ref,
              gcol_ref, grow_ref, ow_ref, o_ref, state_ref):
    group = pl.program_id(0) % (N_HEADS // GDN_HB)
    t = pl.program_id(1)

    @pl.when(t == 0)
    def _():
        state_ref[...] = jnp.zeros_like(state_ref)

    gates = gcol_ref[...]
    lane = lax.broadcasted_iota(jnp.int32, gates.shape, 1)
    n_chunks = SEQ_TILE // CHUNK
    n_prob = GDN_HB * n_chunks

    qs, ks, vs, betas, gcs, grows = [], [], [], [], [], []
    for hh in range(GDN_HB):
        head = group * GDN_HB + hh
        qs.append(_l2norm(_short_conv_silu(q_ref[hh], qh_ref[hh], wq_ref[hh], t)) * (HEAD_DIM ** -0.5))
        ks.append(_l2norm(_short_conv_silu(k_ref[hh], kh_ref[hh], wk_ref[hh], t)))
        vs.append(_short_conv_silu(v_ref[hh], vh_ref[hh], wv_ref[hh], t))
        betas.append(jnp.sum(jnp.where(lane == head, gates, 0.0), axis=-1, keepdims=True))
        gcs.append(jnp.sum(jnp.where(lane == head + N_HEADS, gates, 0.0), axis=-1, keepdims=True))
        grow = grow_ref[hh]
        grows.extend(grow[:, c * CHUNK:(c + 1) * CHUNK] for c in range(n_chunks))

    def chunked(parts):
        return jnp.stack(parts, axis=0).reshape(n_prob, CHUNK, parts[0].shape[-1])

    lhs, o_intra, s_add, decay = _gdn_chunk_terms(
        chunked(qs), chunked(ks), chunked(vs), chunked(betas), chunked(gcs), jnp.stack(grows, axis=0))
    lhs = lhs.reshape(GDN_HB, n_chunks, CHUNK + HEAD_DIM, HEAD_DIM)
    o_intra = o_intra.reshape(GDN_HB, n_chunks, CHUNK, HEAD_DIM)
    s_add = s_add.reshape(GDN_HB, n_chunks, HEAD_DIM, HEAD_DIM)
    decay = decay.reshape(GDN_HB, n_chunks, 1, 1)

    state = state_ref[...]
    ow = ow_ref[...]
    for c in range(n_chunks):
        prod = _bmm(lhs[:, c], state)
        o_c = prod[:, :CHUNK] + o_intra[:, c]
        state = state * decay[:, c] - prod[:, CHUNK:] + s_add[:, c]
        o_n = o_c * lax.rsqrt(jnp.mean(o_c * o_c, axis=-1, keepdims=True) + EPS) * ow
        for hh in range(GDN_HB):
            zc = z_ref[hh, c * CHUNK:(c + 1) * CHUNK, :].astype(F32)
            o_ref[c * CHUNK:(c + 1) * CHUNK, hh * HEAD_DIM:(hh + 1) * HEAD_DIM] = (
                o_n[hh] * (zc * jax.nn.sigmoid(zc))).astype(o_ref.dtype)
    state_ref[...] = state


def _gated_delta(p3, conv_w, gcol, grow, o_norm_w, bsz, seq):
    nt = seq // SEQ_TILE
    halos_per_tile = SEQ_TILE // CONV_HALO
    n_groups = N_HEADS // GDN_HB
    cw = conv_w.astype(F32).reshape(CONV_K, 3 * N_HEADS, HEAD_DIM).transpose(1, 0, 2)
    grow3 = grow.reshape(N_HEADS, 1, bsz * seq)

    def cur(blk):
        return lambda bg, t: (blk // GDN_HB + bg % n_groups, (bg // n_groups) * nt + t, 0)

    def halo(blk):
        return lambda bg, t: (
            blk // GDN_HB + bg % n_groups, jnp.maximum(((bg // n_groups) * nt + t) * halos_per_tile - 1, 0), 0)

    def cwmap(off):
        return lambda bg, t: (off // GDN_HB + bg % n_groups, 0, 0)

    tile = (GDN_HB, SEQ_TILE, HEAD_DIM)
    htile = (GDN_HB, CONV_HALO, HEAD_DIM)
    wtile = (GDN_HB, CONV_K, HEAD_DIM)
    return pl.pallas_call(
        _gdn_body,
        out_shape=jax.ShapeDtypeStruct((bsz * seq, WIDTH), BF16),
        grid=(bsz * n_groups, nt),
        in_specs=[
            pl.BlockSpec(tile, cur(BLK_DN_Q)),
            pl.BlockSpec(tile, cur(BLK_DN_K)),
            pl.BlockSpec(tile, cur(BLK_DN_V)),
            pl.BlockSpec(htile, halo(BLK_DN_Q)),
            pl.BlockSpec(htile, halo(BLK_DN_K)),
            pl.BlockSpec(htile, halo(BLK_DN_V)),
            pl.BlockSpec(wtile, cwmap(0)),
            pl.BlockSpec(wtile, cwmap(N_HEADS)),
            pl.BlockSpec(wtile, cwmap(2 * N_HEADS)),
            pl.BlockSpec(tile, cur(BLK_DN_Z)),
            pl.BlockSpec((SEQ_TILE, HEAD_DIM), lambda bg, t: ((bg // n_groups) * nt + t, 0)),
            pl.BlockSpec((GDN_HB, 1, SEQ_TILE), lambda bg, t: (bg % n_groups, 0, (bg // n_groups) * nt + t)),
            pl.BlockSpec((1, HEAD_DIM), lambda bg, t: (0, 0)),
        ],
        out_specs=pl.BlockSpec(
            (SEQ_TILE, GDN_HB * HEAD_DIM), lambda bg, t: ((bg // n_groups) * nt + t, bg % n_groups)),
        scratch_shapes=[pltpu.VMEM((GDN_HB, HEAD_DIM, HEAD_DIM), F32)],
        compiler_params=_cparams(("parallel", "arbitrary")),
        name="gated_delta",
    )(p3, p3, p3, p3, p3, p3, cw, cw, cw, p3, gcol, grow3, o_norm_w.astype(F32).reshape(1, HEAD_DIM))


def _outproj_body(ya_ref, yb_ref, wa_ref, wb_ref, x_ref, o_ref):
    acc = jnp.dot(ya_ref[...], wa_ref[...].astype(BF16), preferred_element_type=F32)
    acc = acc + jnp.dot(yb_ref[...], wb_ref[...].astype(BF16), preferred_element_type=F32)
    o_ref[...] = x_ref[...] + acc


def _outproj(ya, yb, w_out, x2d):
    t, d = x2d.shape
    tm = min(1024, t)
    tn = min(512, d)
    return pl.pallas_call(
        _outproj_body,
        out_shape=jax.ShapeDtypeStruct((t, d), F32),
        grid=(t // tm, d // tn),
        in_specs=[
            pl.BlockSpec((tm, WIDTH), lambda i, j: (i, 0)),
            pl.BlockSpec((tm, WIDTH), lambda i, j: (i, 0)),
            pl.BlockSpec((WIDTH, tn), lambda i, j: (0, j)),
            pl.BlockSpec((WIDTH, tn), lambda i, j: (1, j)),
            pl.BlockSpec((tm, tn), lambda i, j: (i, j)),
        ],
        out_specs=pl.BlockSpec((tm, tn), lambda i, j: (i, j)),
        compiler_params=_cparams(("parallel", "arbitrary")),
        name="outproj",
    )(ya, yb, w_out, w_out, x2d)


def _router_body(x_ref, nw_ref, wr_ref, r_ref):
    x = x_ref[...]
    h = x * lax.rsqrt(jnp.mean(x * x, axis=-1, keepdims=True) + EPS) * nw_ref[...]
    logits = jnp.dot(h, wr_ref[...], precision=lax.Precision.HIGHEST, preferred_element_type=F32)
    lane = lax.broadcasted_iota(jnp.int32, logits.shape, 1).astype(F32)
    big = float(HEAD_DIM)

    def first_argmax(vals, vmax):
        return jnp.min(jnp.where(vals == vmax, lane, big), axis=-1, keepdims=True)

    gl = jnp.where(lane < N_GROUPS, logits, NEG)
    gmax = jnp.max(gl, axis=-1, keepdims=True)
    gidx = first_argmax(gl, gmax)
    grp_w = 1.0 / jnp.sum(jnp.exp(gl - gmax), axis=-1, keepdims=True)
    lo = N_GROUPS + gidx * EXPERTS_PER_GROUP
    el = jnp.where((lane >= lo) & (lane < lo + EXPERTS_PER_GROUP), logits, NEG)
    m1 = jnp.max(el, axis=-1, keepdims=True)
    i1 = first_argmax(el, m1)
    el2 = jnp.where(lane == i1, NEG, el)
    m2 = jnp.max(el2, axis=-1, keepdims=True)
    i2 = first_argmax(el2, m2)
    r = jnp.exp(m2 - m1)
    w1 = 1.0 / (1.0 + r)
    w2 = r / (1.0 + r)
    out = jnp.where(lane == 0, i1 - N_GROUPS, 0.0)
    out = jnp.where(lane == 1, i2 - N_GROUPS, out)
    out = jnp.where(lane == 2, grp_w * w1, out)
    out = jnp.where(lane == 3, grp_w * w2, out)
    r_ref[...] = out


def _norm_router(x1, norm_w, w_group, w_router):
    t, d = x1.shape
    tm = min(256, t)
    wr = jnp.concatenate([w_group, w_router], axis=1).astype(F32)
    wr = jnp.pad(wr, ((0, 0), (0, HEAD_DIM - wr.shape[1])))
    return pl.pallas_call(
        _router_body,
        out_shape=jax.ShapeDtypeStruct((t, HEAD_DIM), F32),
        grid=(t // tm,),
        in_specs=[
            pl.BlockSpec((tm, d), lambda i: (i, 0)),
            pl.BlockSpec((1, d), lambda i: (0, 0)),
            pl.BlockSpec((d, HEAD_DIM), lambda i: (0, 0)),
        ],
        out_specs=pl.BlockSpec((tm, HEAD_DIM), lambda i: (i, 0)),
        compiler_params=_cparams(("parallel",)),
        name="norm_router",
    )(x1, norm_w.astype(F32).reshape(1, d), wr)


def _row_gather_start(src_hbm, idx_ref, dst, sem, row0, n_rows):
    def body(r, carry):
        rr = row0 + r
        pltpu.make_async_copy(src_hbm.at[pl.ds(idx_ref[0, rr], 1)], dst.at[pl.ds(rr, 1)], sem).start()
        return carry

    lax.fori_loop(0, n_rows, body, 0, unroll=8)


def _row_gather_wait(dst, sem):
    pltpu.make_async_copy(dst, dst, sem).wait()


MOE_BM = 256


def _dispatch_body(nblk_ref, tok_ref, tokn_ref, x_hbm, nw_ref, o_ref, xbuf, sem):
    i = pl.program_id(0)
    nblk = nblk_ref[0]
    slot = i % 2

    @pl.when((i == 0) & (nblk > 0))
    def _():
        _row_gather_start(x_hbm, tok_ref, xbuf.at[0], sem.at[0], 0, MOE_BM)

    @pl.when(i + 1 < nblk)
    def _():
        _row_gather_start(x_hbm, tokn_ref, xbuf.at[1 - slot], sem.at[1 - slot], 0, MOE_BM)

    @pl.when(i < nblk)
    def _():
        _row_gather_wait(xbuf.at[slot], sem.at[slot])
        x = xbuf[slot]
        h = x * lax.rsqrt(jnp.mean(x * x, axis=-1, keepdims=True) + EPS) * nw_ref[...]
        o_ref[...] = h.astype(o_ref.dtype)

    @pl.when(i >= nblk)
    def _():
        o_ref[...] = jnp.zeros_like(o_ref)


def _dispatch_gather(x1, norm_w, nblk, slot_tok):
    t, d = x1.shape
    n_blocks = slot_tok.shape[0] // MOE_BM
    tok_arr = slot_tok.reshape(n_blocks, 1, MOE_BM)
    smem_blk = functools.partial(pl.BlockSpec, (None, 1, MOE_BM), memory_space=pltpu.SMEM)
    grid_spec = pltpu.PrefetchScalarGridSpec(
        num_scalar_prefetch=1,
        grid=(n_blocks,),
        in_specs=[
            smem_blk(lambda i, nb: (i, 0, 0)),
            smem_blk(lambda i, nb: (jnp.minimum(i + 1, n_blocks - 1), 0, 0)),
            pl.BlockSpec(memory_space=pl.ANY),
            pl.BlockSpec((1, d), lambda i, nb: (0, 0)),
        ],
        out_specs=pl.BlockSpec((MOE_BM, d), lambda i, nb: (i, 0)),
        scratch_shapes=[pltpu.VMEM((2, MOE_BM, d), F32), pltpu.SemaphoreType.DMA((2,))],
    )
    return pl.pallas_call(
        _dispatch_body,
        out_shape=jax.ShapeDtypeStruct((n_blocks * MOE_BM, d), BF16),
        grid_spec=grid_spec,
        compiler_params=_cparams(("arbitrary",)),
        name="moe_dispatch",
    )(nblk, tok_arr, tok_arr, x1, norm_w.astype(F32).reshape(1, d))


def _ffn_body(nblk_ref, blke_ref, x_ref, wg_ref, wu_ref, wd_ref, y_ref, wg_bf, wu_bf, wd_bf):
    i = pl.program_id(0)
    nblk = nblk_ref[0]
    new_expert = (i == 0) | (blke_ref[i] != blke_ref[jnp.maximum(i - 1, 0)])

    @pl.when((i < nblk) & new_expert)
    def _():
        wg_bf[...] = wg_ref[...].astype(BF16)
        wu_bf[...] = wu_ref[...].astype(BF16)
        wd_bf[...] = wd_ref[...].astype(BF16)

    @pl.when(i < nblk)
    def _():
        x = x_ref[...]
        g = jnp.dot(x, wg_bf[...], preferred_element_type=F32)
        u = jnp.dot(x, wu_bf[...], preferred_element_type=F32)
        hid = (g * jax.nn.sigmoid(g) * u).astype(BF16)
        y_ref[...] = jnp.dot(hid, wd_bf[...], preferred_element_type=F32)

    @pl.when(i >= nblk)
    def _():
        y_ref[...] = jnp.zeros_like(y_ref)


def _expert_ffn(xs, w_gate, w_up, w_down, nblk, blk_e):
    n_slots, d = xs.shape
    n_blocks = blk_e.shape[0]
    single = pl.Buffered(1)
    grid_spec = pltpu.PrefetchScalarGridSpec(
        num_scalar_prefetch=2,
        grid=(n_blocks,),
        in_specs=[
            pl.BlockSpec((MOE_BM, d), lambda i, nb, be: (i, 0)),
            pl.BlockSpec((None, d, D_EXPERT), lambda i, nb, be: (be[i], 0, 0), pipeline_mode=single),
            pl.BlockSpec((None, d, D_EXPERT), lambda i, nb, be: (be[i], 0, 0), pipeline_mode=single),
            pl.BlockSpec((None, D_EXPERT, d), lambda i, nb, be: (be[i], 0, 0), pipeline_mode=single),
        ],
        out_specs=pl.BlockSpec((MOE_BM, d), lambda i, nb, be: (i, 0)),
        scratch_shapes=[
            pltpu.VMEM((d, D_EXPERT), BF16),
            pltpu.VMEM((d, D_EXPERT), BF16),
            pltpu.VMEM((D_EXPERT, d), BF16),
        ],
    )
    return pl.pallas_call(
        _ffn_body,
        out_shape=jax.ShapeDtypeStruct((n_slots, d), F32),
        grid_spec=grid_spec,
        compiler_params=_cparams(("arbitrary",)),
        name="expert_ffn",
    )(nblk, blk_e, xs, w_gate, w_up, w_down)


def _combine_body(d0_ref, d1_ref, d0n_ref, d1n_ref, x_ref, r_ref, y_hbm, o_ref, ybuf, sem):
    i = pl.program_id(0)
    n = pl.num_programs(0)
    slot = i % 2

    @pl.when(i == 0)
    def _():
        _row_gather_start(y_hbm, d0_ref, ybuf.at[0, 0], sem.at[0], 0, COMBINE_TM)
        _row_gather_start(y_hbm, d1_ref, ybuf.at[0, 1], sem.at[0], 0, COMBINE_TM)

    @pl.when(i + 1 < n)
    def _():
        _row_gather_start(y_hbm, d0n_ref, ybuf.at[1 - slot, 0], sem.at[1 - slot], 0, COMBINE_TM)
        _row_gather_start(y_hbm, d1n_ref, ybuf.at[1 - slot, 1], sem.at[1 - slot], 0, COMBINE_TM)

    _row_gather_wait(ybuf.at[slot], sem.at[slot])
    route = r_ref[...]
    o_ref[...] = x_ref[...] + route[:, 2:3] * ybuf[slot, 0] + route[:, 3:4] * ybuf[slot, 1]


def _combine(x1, route, y, dest0, dest1):
    t, d = x1.shape
    tm = min(COMBINE_TM, t)
    assert tm == COMBINE_TM
    n = t // tm
    d0 = dest0.reshape(n, 1, tm)
    d1 = dest1.reshape(n, 1, tm)
    smem_blk = functools.partial(pl.BlockSpec, (None, 1, tm), memory_space=pltpu.SMEM)
    nxt = lambda i: (jnp.minimum(i + 1, n - 1), 0, 0)
    return pl.pallas_call(
        _combine_body,
        out_shape=jax.ShapeDtypeStruct((t, d), F32),
        grid=(n,),
        in_specs=[
            smem_blk(lambda i: (i, 0, 0)),
            smem_blk(lambda i: (i, 0, 0)),
            smem_blk(nxt),
            smem_blk(nxt),
            pl.BlockSpec((tm, d), lambda i: (i, 0)),
            pl.BlockSpec((tm, HEAD_DIM), lambda i: (i, 0)),
            pl.BlockSpec(memory_space=pl.ANY),
        ],
        out_specs=pl.BlockSpec((tm, d), lambda i: (i, 0)),
        scratch_shapes=[pltpu.VMEM((2, 2, tm, d), F32), pltpu.SemaphoreType.DMA((2,))],
        compiler_params=_cparams(("arbitrary",)),
        name="moe_combine",
    )(d0, d1, d0, d1, x1, route, y)


def _dispatch_plan(route, t):
    expert = route[:, 0:2].astype(jnp.int32)
    n_assign = 2 * t
    flat_e = expert.reshape(n_assign)
    order = jnp.argsort(flat_e).astype(jnp.int32)
    e_s = flat_e[order]
    below = jnp.sum(flat_e[None, :] < jnp.arange(N_EXPERTS + 1, dtype=jnp.int32)[:, None], axis=1)
    start = below[:-1].astype(jnp.int32)
    counts = (below[1:] - below[:-1]).astype(jnp.int32)
    padded = (counts + MOE_BM - 1) // MOE_BM * MOE_BM
    pad_end = jnp.cumsum(padded)
    pad_start = pad_end - padded
    dest_sorted = pad_start[e_s] + (jnp.arange(n_assign, dtype=jnp.int32) - start[e_s])
    _, dest = lax.sort_key_val(order, dest_sorted)
    dest = dest.reshape(t, 2)
    n_blocks = n_assign // MOE_BM + N_EXPERTS
    nblk = (pad_end[-1] // MOE_BM).astype(jnp.int32).reshape(1)
    blk_start = jnp.arange(n_blocks, dtype=jnp.int32) * MOE_BM
    blk_e = jnp.minimum(
        jnp.sum(pad_end[None, :] <= blk_start[:, None], axis=1), N_EXPERTS - 1).astype(jnp.int32)
    slot_e = jnp.repeat(blk_e, MOE_BM)
    off = jnp.arange(n_blocks * MOE_BM, dtype=jnp.int32) - pad_start[slot_e]
    src = jnp.clip(start[slot_e] + off, 0, n_assign - 1)
    slot_tok = jnp.where(off < counts[slot_e], order[src] // 2, 0).astype(jnp.int32)
    last_e = blk_e[jnp.maximum(nblk[0] - 1, 0)]
    blk_e = jnp.where(jnp.arange(n_blocks) < nblk[0], blk_e, last_e)
    return nblk, blk_e, slot_tok, dest[:, 0], dest[:, 1]


def _layer(x, norm1_w, w_in, q_norm_w, k_norm_w, rel_bias, conv_w, a_log, dt_bias,
           o_norm_w, w_out, norm2_w, w_group, w_router, w_gate, w_up, w_down):
    bsz, seq, d = x.shape
    t = bsz * seq
    assert seq % SEQ_TILE == 0 and d % HEAD_DIM == 0 and t % COMBINE_TM == 0
    assert w_in.shape[1] == MAIN_COLS + 2 * N_HEADS
    x2d = x.reshape(t, d)

    w_in_t = jnp.swapaxes(w_in, 0, 1)
    h1 = _rmsnorm(x2d, norm1_w.astype(F32), BF16)
    p3 = _inproj(h1, w_in_t)
    gcol, grow = _gates(h1, w_in_t[MAIN_COLS:], a_log, dt_bias)

    ya = _band_attention(p3, _attn_bias_tile(rel_bias), q_norm_w, k_norm_w, bsz, seq)
    yb = _gated_delta(p3, conv_w, gcol, grow, o_norm_w, bsz, seq)
    x1 = _outproj(ya, yb, w_out, x2d)

    route = _norm_router(x1, norm2_w, w_group, w_router)
    nblk, blk_e, slot_tok, dest0, dest1 = _dispatch_plan(route, t)
    xs = _dispatch_gather(x1, norm2_w, nblk, slot_tok)
    y = _expert_ffn(xs, w_gate, w_up, w_down, nblk, blk_e)
    out = _combine(x1, route, y, dest0, dest1)
    return out.reshape(bsz, seq, d)


def kernel(x, norm1_w, w_in, q_norm_w, k_norm_w, rel_bias, conv_w, a_log, dt_bias, o_norm_w, w_out,
           norm2_w, w_group, w_router, w_gate, w_up, w_down):
    for l in range(norm1_w.shape[0]):
        x = _layer(x, norm1_w[l], w_in[l], q_norm_w[l], k_norm_w[l], rel_bias[l], conv_w[l], a_log[l],
                   dt_bias[l], o_norm_w[l], w_out[l], norm2_w[l], w_group[l], w_router[l], w_gate[l],
                   w_up[l], w_down[l])
    return x
```

```python
import functools

import jax
import jax.numpy as jnp
import numpy as np
from jax import lax
from jax.experimental import pallas as pl
from jax.experimental.pallas import tpu as pltpu

F32 = jnp.float32
BF16 = jnp.bfloat16

CHUNK = 64
HEAD_DIM = 128
N_HEADS = 16
WIDTH = N_HEADS * HEAD_DIM
LEFT_CHUNKS = 8
REL_CLIP = 256
CONV_K = 4
N_GROUPS = 8
EXPERTS_PER_GROUP = 8
N_EXPERTS = N_GROUPS * EXPERTS_PER_GROUP
D_EXPERT = 512
EPS = 1e-6
NEG = -1e30

BLK_ATT_Q, BLK_ATT_K, BLK_ATT_V = 0, 16, 32
BLK_DN_Q, BLK_DN_K, BLK_DN_V, BLK_DN_Z = 48, 64, 80, 96
N_MAIN_BLKS = 112
MAIN_COLS = N_MAIN_BLKS * HEAD_DIM

SEQ_TILE = 512
COMBINE_TM = 256

V7X_VMEM_LIMIT = 56 * 1024 * 1024


def _cparams(sem, vmem=V7X_VMEM_LIMIT):
    return pltpu.CompilerParams(dimension_semantics=sem, vmem_limit_bytes=vmem)


def _dot_nt(a, b):
    return lax.dot_general(a, b, (((1,), (1,)), ((), ())), preferred_element_type=F32)


def _rmsnorm_body(x_ref, w_ref, o_ref):
    x = x_ref[...]
    ms = jnp.mean(x * x, axis=-1, keepdims=True)
    o_ref[...] = (x * lax.rsqrt(ms + EPS) * w_ref[...]).astype(o_ref.dtype)


def _rmsnorm(x2d, w, out_dtype):
    t, d = x2d.shape
    tm = min(256, t)
    return pl.pallas_call(
        _rmsnorm_body,
        out_shape=jax.ShapeDtypeStruct((t, d), out_dtype),
        grid=(t // tm,),
        in_specs=[pl.BlockSpec((tm, d), lambda i: (i, 0)), pl.BlockSpec((1, d), lambda i: (0, 0))],
        out_specs=pl.BlockSpec((tm, d), lambda i: (i, 0)),
        compiler_params=_cparams(("parallel",)),
        name="rmsnorm",
    )(x2d, w.reshape(1, d))


def _inproj_body(h_ref, wt_ref, o_ref):
    acc = _dot_nt(h_ref[...], wt_ref[...].astype(BF16))
    for c in range(o_ref.shape[0]):
        o_ref[c] = acc[:, c * HEAD_DIM:(c + 1) * HEAD_DIM].astype(o_ref.dtype)


def _inproj(h, w_in_t):
    t, d = h.shape
    tm = min(1024, t)
    tn = 512
    nb = tn // HEAD_DIM
    return pl.pallas_call(
        _inproj_body,
        out_shape=jax.ShapeDtypeStruct((N_MAIN_BLKS, t, HEAD_DIM), BF16),
        grid=(t // tm, MAIN_COLS // tn),
        in_specs=[pl.BlockSpec((tm, d), lambda i, j: (i, 0)), pl.BlockSpec((tn, d), lambda i, j: (j, 0))],
        out_specs=pl.BlockSpec((nb, tm, HEAD_DIM), lambda i, j: (j, i, 0)),
        compiler_params=_cparams(("parallel", "arbitrary")),
        name="inproj",
    )(h, w_in_t)


def _chunk_cumsum_rows(x):
    row = lax.broadcasted_iota(jnp.int32, x.shape, 0) % CHUNK
    shift = 1
    while shift < CHUNK:
        x = x + jnp.where(row >= shift, pltpu.roll(x, shift, 0), 0.0)
        shift *= 2
    return x


def _gates_body(h_ref, wt_ref, a_ref, dtb_ref, col_ref, row_ref):
    p = _dot_nt(h_ref[...], wt_ref[...].astype(BF16))
    lane = lax.broadcasted_iota(jnp.int32, p.shape, 1)
    beta = jax.nn.sigmoid(p)
    z = p + dtb_ref[...]
    softplus = jnp.maximum(z, 0.0) + jnp.log1p(jnp.exp(-jnp.abs(z)))
    g = -a_ref[...] * softplus
    out = jnp.where(lane < N_HEADS, beta, _chunk_cumsum_rows(g))
    col_ref[...] = out
    row_ref[...] = out.T[N_HEADS:2 * N_HEADS, :]


def _gates(h, w_small_t, a_log, dt_bias):
    t, d = h.shape
    tm = min(1024, t)
    pad = HEAD_DIM - 2 * N_HEADS
    w = jnp.pad(w_small_t, ((0, pad), (0, 0)))
    a = jnp.pad(jnp.exp(a_log.astype(F32)), (N_HEADS, pad)).reshape(1, HEAD_DIM)
    dtb = jnp.pad(dt_bias.astype(F32), (N_HEADS, pad)).reshape(1, HEAD_DIM)
    return pl.pallas_call(
        _gates_body,
        out_shape=(jax.ShapeDtypeStruct((t, HEAD_DIM), F32), jax.ShapeDtypeStruct((N_HEADS, t), F32)),
        grid=(t // tm,),
        in_specs=[
            pl.BlockSpec((tm, d), lambda i: (i, 0)),
            pl.BlockSpec((HEAD_DIM, d), lambda i: (0, 0)),
            pl.BlockSpec((1, HEAD_DIM), lambda i: (0, 0)),
            pl.BlockSpec((1, HEAD_DIM), lambda i: (0, 0)),
        ],
        out_specs=(pl.BlockSpec((tm, HEAD_DIM), lambda i: (i, 0)), pl.BlockSpec((N_HEADS, tm), lambda i: (0, i))),
        compiler_params=_cparams(("parallel",)),
        name="gates",
    )(h, w, a, dtb)


ATT_SUB = 128
ATT_WIN = ATT_SUB + LEFT_CHUNKS * CHUNK


def _attn_bias_tile(rel_bias):
    period = ATT_SUB + ATT_WIN
    offs = np.arange(period)
    key_minus_query = np.where(offs < ATT_WIN, offs, offs - period)
    rel = LEFT_CHUNKS * CHUNK - key_minus_query
    diag_vals = rel_bias[:, np.clip(rel, -REL_CLIP, REL_CLIP) + REL_CLIP].astype(F32)
    flat = jnp.tile(diag_vals, (1, ATT_SUB))[:, :ATT_SUB * (period - 1)]
    bias = flat.reshape(-1, ATT_SUB, period - 1)[:, :, :ATT_WIN]
    qc = np.arange(ATT_SUB)[:, None] // CHUNK
    kc = np.arange(ATT_WIN)[None, :] // CHUNK
    allowed = (kc >= qc) & (kc <= qc + LEFT_CHUNKS)
    return jnp.where(allowed[None], bias, NEG)


def _head_rmsnorm(x, w):
    xf = x.astype(F32)
    return xf * lax.rsqrt(jnp.mean(xf * xf, axis=-1, keepdims=True) + EPS) * w


def _attn_body(q_ref, kp_ref, kc_ref, vp_ref, vc_ref, bias_ref, qw_ref, kw_ref, o_ref):
    t = pl.program_id(1)
    q = _head_rmsnorm(q_ref[...], qw_ref[...]).astype(BF16)
    k = jnp.concatenate(
        [_head_rmsnorm(kp_ref[...], kw_ref[...]), _head_rmsnorm(kc_ref[...], kw_ref[...])], axis=0
    ).astype(BF16)
    v = jnp.concatenate([vp_ref[...], vc_ref[...]], axis=0)
    n_sub = SEQ_TILE // ATT_SUB
    qb = q.reshape(n_sub, ATT_SUB, HEAD_DIM)
    kb = jnp.stack([k[p * ATT_SUB:p * ATT_SUB + ATT_WIN] for p in range(n_sub)], axis=0)
    vb = jnp.stack([v[p * ATT_SUB:p * ATT_SUB + ATT_WIN] for p in range(n_sub)], axis=0)
    s = jnp.einsum("pqd,pkd->pqk", qb, kb, preferred_element_type=F32) + bias_ref[...][None]
    col = lax.broadcasted_iota(jnp.int32, (n_sub, ATT_SUB, ATT_WIN), 2)
    sub = lax.broadcasted_iota(jnp.int32, (n_sub, ATT_SUB, ATT_WIN), 0)
    first_valid = jnp.where(t > 0, 0, SEQ_TILE) - sub * jnp.where(t > 0, 0, ATT_SUB)
    s = jnp.where(col >= first_valid, s, NEG)
    m = jnp.max(s, axis=-1, keepdims=True)
    e = jnp.exp(s - m)
    l = jnp.sum(e, axis=-1, keepdims=True)
    o = jnp.einsum("pqk,pkd->pqd", e.astype(BF16), vb, preferred_element_type=F32) / l
    o_ref[...] = o.reshape(SEQ_TILE, HEAD_DIM).astype(o_ref.dtype)


def _band_attention(p3, bias_tile, q_norm_w, k_norm_w, bsz, seq):
    nt = seq // SEQ_TILE
    qw = (q_norm_w.astype(F32) * (HEAD_DIM ** -0.5)).reshape(1, HEAD_DIM)
    kw = k_norm_w.astype(F32).reshape(1, HEAD_DIM)

    def cur(blk):
        return lambda bh, t: (blk + bh % N_HEADS, (bh // N_HEADS) * nt + t, 0)

    def prev(blk):
        return lambda bh, t: (blk + bh % N_HEADS, (bh // N_HEADS) * nt + jnp.maximum(t - 1, 0), 0)

    tile = (None, SEQ_TILE, HEAD_DIM)
    return pl.pallas_call(
        _attn_body,
        out_shape=jax.ShapeDtypeStruct((bsz * seq, WIDTH), BF16),
        grid=(bsz * N_HEADS, nt),
        in_specs=[
            pl.BlockSpec(tile, cur(BLK_ATT_Q)),
            pl.BlockSpec(tile, prev(BLK_ATT_K)),
            pl.BlockSpec(tile, cur(BLK_ATT_K)),
            pl.BlockSpec(tile, prev(BLK_ATT_V)),
            pl.BlockSpec(tile, cur(BLK_ATT_V)),
            pl.BlockSpec((None, ATT_SUB, ATT_WIN), lambda bh, t: (bh % N_HEADS, 0, 0)),
            pl.BlockSpec((1, HEAD_DIM), lambda bh, t: (0, 0)),
            pl.BlockSpec((1, HEAD_DIM), lambda bh, t: (0, 0)),
        ],
        out_specs=pl.BlockSpec((SEQ_TILE, HEAD_DIM), lambda bh, t: ((bh // N_HEADS) * nt + t, bh % N_HEADS)),
        compiler_params=_cparams(("parallel", "arbitrary")),
        name="band_attention",
    )(p3, p3, p3, p3, p3, bias_tile, qw, kw)


CONV_HALO = 16


def _short_conv_silu(cur, halo, w, t):
    cur = cur.astype(F32)
    halo = jnp.where(t > 0, halo.astype(F32), 0.0)
    cat = jnp.concatenate([halo, cur], axis=0)
    y = cur * w[CONV_K - 1:CONV_K]
    for i in range(CONV_K - 1):
        shift = CONV_K - 1 - i
        y = y + pltpu.roll(cat, shift, 0)[CONV_HALO:] * w[i:i + 1]
    return y * jax.nn.sigmoid(y)


def _l2norm(x):
    return x * lax.rsqrt(jnp.sum(x * x, axis=-1, keepdims=True) + EPS)


def _bdot(a, b):
    return jnp.dot(a.astype(BF16), b.astype(BF16), preferred_element_type=F32)


def _bdot_nt(a, b):
    return lax.dot_general(a.astype(BF16), b.astype(BF16), (((1,), (1,)), ((), ())), preferred_element_type=F32)


GDN_HB = 8


def _bmm(a, b):
    return jnp.einsum("gmk,gkn->gmn", a.astype(BF16), b.astype(BF16), preferred_element_type=F32)


def _bmm_nt(a, b):
    return jnp.einsum("gmk,gnk->gmn", a.astype(BF16), b.astype(BF16), preferred_element_type=F32)


def _gdn_chunk_terms(q, k, v, beta, gc, grow):
    ri = lax.broadcasted_iota(jnp.int32, (1, CHUNK, CHUNK), 1)
    ci = lax.broadcasted_iota(jnp.int32, (1, CHUNK, CHUNK), 2)
    gamma = jnp.exp(jnp.where(ri >= ci, gc - grow, NEG))
    kb = k * beta
    kqk = _bmm_nt(jnp.concatenate([kb, q], axis=1), k)
    m = jnp.where(ri > ci, kqk[:, :CHUNK] * gamma, 0.0)
    att = kqk[:, CHUNK:] * gamma
    inv = (ri == ci).astype(F32) - m
    pw = _bmm(m, m)
    for _ in range(4):
        both = _bmm(jnp.concatenate([inv, pw], axis=1), pw)
        inv = inv + both[:, :CHUNK]
        pw = both[:, CHUNK:]
    inv = inv + _bmm(inv, pw)
    eg = jnp.exp(gc)
    g_last = gc[:, CHUNK - 1:CHUNK]
    sol = _bmm(inv, jnp.concatenate([v * beta, kb * eg], axis=2))
    att_uw = _bmm(att, sol)
    k_dec = k * jnp.exp(g_last - gc)
    kd_uw = _bmm(jnp.swapaxes(k_dec, 1, 2), sol)
    lhs = jnp.concatenate([q * eg - att_uw[:, :, HEAD_DIM:], kd_uw[:, :, HEAD_DIM:]], axis=1)
    return lhs, att_uw[:, :, :HEAD_DIM], kd_uw[:, :, :HEAD_DIM], jnp.exp(g_last)


def _gdn_body(q_ref, k_ref, v_ref, qh_ref, kh_ref, vh_ref, wq_ref, wk_ref, wv_ref, z_ref,
              gcol_ref, grow_ref, ow_ref, o_ref, state_ref):
    group = pl.program_id(0) % (N_HEADS // GDN_HB)
    t = pl.program_id(1)

    @pl.when(t == 0)
    def _():
        state_ref[...] = jnp.zeros_like(state_ref)

    gates = gcol_ref[...]
    lane = lax.broadcasted_iota(jnp.int32, gates.shape, 1)
    n_chunks = SEQ_TILE // CHUNK
    n_prob = GDN_HB * n_chunks

    qs, ks, vs, betas, gcs, grows = [], [], [], [], [], []
    for hh in range(GDN_HB):
        head = group * GDN_HB + hh
        qs.append(_l2norm(_short_conv_silu(q_ref[hh], qh_ref[hh], wq_ref[hh], t)) * (HEAD_DIM ** -0.5))
        ks.append(_l2norm(_short_conv_silu(k_ref[hh], kh_ref[hh], wk_ref[hh], t)))
        vs.append(_short_conv_silu(v_ref[hh], vh_ref[hh], wv_ref[hh], t))
        betas.append(jnp.sum(jnp.where(lane == head, gates, 0.0), axis=-1, keepdims=True))
        gcs.append(jnp.sum(jnp.where(lane == head + N_HEADS, gates, 0.0), axis=-1, keepdims=True))
        grow = grow_ref[hh]
        grows.extend(grow[:, c * CHUNK:(c + 1) * CHUNK] for c in range(n_chunks))

    def chunked(parts):
        return jnp.stack(parts, axis=0).reshape(n_prob, CHUNK, parts[0].shape[-1])

    lhs, o_intra, s_add, decay = _gdn_chunk_terms(
        chunked(qs), chunked(ks), chunked(vs), chunked(betas), chunked(gcs), jnp.stack(grows, axis=0))
    lhs = lhs.reshape(GDN_HB, n_chunks, CHUNK + HEAD_DIM, HEAD_DIM)
    o_intra = o_intra.reshape(GDN_HB, n_chunks, CHUNK, HEAD_DIM)
    s_add = s_add.reshape(GDN_HB, n_chunks, HEAD_DIM, HEAD_DIM)
    decay = decay.reshape(GDN_HB, n_chunks, 1, 1)

    state = state_ref[...]
    ow = ow_ref[...]
    for c in range(n_chunks):
        prod = _bmm(lhs[:, c], state)
        o_c = prod[:, :CHUNK] + o_intra[:, c]
        state = state * decay[:, c] - prod[:, CHUNK:] + s_add[:, c]
        o_n = o_c * lax.rsqrt(jnp.mean(o_c * o_c, axis=-1, keepdims=True) + EPS) * ow
        for hh in range(GDN_HB):
            zc = z_ref[hh, c * CHUNK:(c + 1) * CHUNK, :].astype(F32)
            o_ref[c * CHUNK:(c + 1) * CHUNK, hh * HEAD_DIM:(hh + 1) * HEAD_DIM] = (
                o_n[hh] * (zc * jax.nn.sigmoid(zc))).astype(o_ref.dtype)
    state_ref[...] = state


def _gated_delta(p3, conv_w, gcol, grow, o_norm_w, bsz, seq):
    nt = seq // SEQ_TILE
    halos_per_tile = SEQ_TILE // CONV_HALO
    n_groups = N_HEADS // GDN_HB
    cw = conv_w.astype(F32).reshape(CONV_K, 3 * N_HEADS, HEAD_DIM).transpose(1, 0, 2)
    grow3 = grow.reshape(N_HEADS, 1, bsz * seq)

    def cur(blk):
        return lambda bg, t: (blk // GDN_HB + bg % n_groups, (bg // n_groups) * nt + t, 0)

    def halo(blk):
        return lambda bg, t: (
            blk // GDN_HB + bg % n_groups, jnp.maximum(((bg // n_groups) * nt + t) * halos_per_tile - 1, 0), 0)

    def cwmap(off):
        return lambda bg, t: (off // GDN_HB + bg % n_groups, 0, 0)

    tile = (GDN_HB, SEQ_TILE, HEAD_DIM)
    htile = (GDN_HB, CONV_HALO, HEAD_DIM)
    wtile = (GDN_HB, CONV_K, HEAD_DIM)
    return pl.pallas_call(
        _gdn_body,
        out_shape=jax.ShapeDtypeStruct((bsz * seq, WIDTH), BF16),
        grid=(bsz * n_groups, nt),
        in_specs=[
            pl.BlockSpec(tile, cur(BLK_DN_Q)),
            pl.BlockSpec(tile, cur(BLK_DN_K)),
            pl.BlockSpec(tile, cur(BLK_DN_V)),
            pl.BlockSpec(htile, halo(BLK_DN_Q)),
            pl.BlockSpec(htile, halo(BLK_DN_K)),
            pl.BlockSpec(htile, halo(BLK_DN_V)),
            pl.BlockSpec(wtile, cwmap(0)),
            pl.BlockSpec(wtile, cwmap(N_HEADS)),
            pl.BlockSpec(wtile, cwmap(2 * N_HEADS)),
            pl.BlockSpec(tile, cur(BLK_DN_Z)),
            pl.BlockSpec((SEQ_TILE, HEAD_DIM), lambda bg, t: ((bg // n_groups) * nt + t, 0)),
            pl.BlockSpec((GDN_HB, 1, SEQ_TILE), lambda bg, t: (bg % n_groups, 0, (bg // n_groups) * nt + t)),
            pl.BlockSpec((1, HEAD_DIM), lambda bg, t: (0, 0)),
        ],
        out_specs=pl.BlockSpec(
            (SEQ_TILE, GDN_HB * HEAD_DIM), lambda bg, t: ((bg // n_groups) * nt + t, bg % n_groups)),
        scratch_shapes=[pltpu.VMEM((GDN_HB, HEAD_DIM, HEAD_DIM), F32)],
        compiler_params=_cparams(("parallel", "arbitrary")),
        name="gated_delta",
    )(p3, p3, p3, p3, p3, p3, cw, cw, cw, p3, gcol, grow3, o_norm_w.astype(F32).reshape(1, HEAD_DIM))


def _outproj_body(ya_ref, yb_ref, wa_ref, wb_ref, x_ref, o_ref):
    acc = jnp.dot(ya_ref[...], wa_ref[...].astype(BF16), preferred_element_type=F32)
    acc = acc + jnp.dot(yb_ref[...], wb_ref[...].astype(BF16), preferred_element_type=F32)
    o_ref[...] = x_ref[...] + acc


def _outproj(ya, yb, w_out, x2d):
    t, d = x2d.shape
    tm = min(1024, t)
    tn = min(512, d)
    return pl.pallas_call(
        _outproj_body,
        out_shape=jax.ShapeDtypeStruct((t, d), F32),
        grid=(t // tm, d // tn),
        in_specs=[
            pl.BlockSpec((tm, WIDTH), lambda i, j: (i, 0)),
            pl.BlockSpec((tm, WIDTH), lambda i, j: (i, 0)),
            pl.BlockSpec((WIDTH, tn), lambda i, j: (0, j)),
            pl.BlockSpec((WIDTH, tn), lambda i, j: (1, j)),
            pl.BlockSpec((tm, tn), lambda i, j: (i, j)),
        ],
        out_specs=pl.BlockSpec((tm, tn), lambda i, j: (i, j)),
        compiler_params=_cparams(("parallel", "arbitrary")),
        name="outproj",
    )(ya, yb, w_out, w_out, x2d)


def _router_body(x_ref, nw_ref, wr_ref, h_ref, r_ref):
    x = x_ref[...]
    h = x * lax.rsqrt(jnp.mean(x * x, axis=-1, keepdims=True) + EPS) * nw_ref[...]
    h_ref[...] = h
    logits = jnp.dot(h, wr_ref[...], precision=lax.Precision.HIGHEST, preferred_element_type=F32)
    lane = lax.broadcasted_iota(jnp.int32, logits.shape, 1).astype(F32)
    big = float(HEAD_DIM)

    def first_argmax(vals, vmax):
        return jnp.min(jnp.where(vals == vmax, lane, big), axis=-1, keepdims=True)

    gl = jnp.where(lane < N_GROUPS, logits, NEG)
    gmax = jnp.max(gl, axis=-1, keepdims=True)
    gidx = first_argmax(gl, gmax)
    grp_w = 1.0 / jnp.sum(jnp.exp(gl - gmax), axis=-1, keepdims=True)
    lo = N_GROUPS + gidx * EXPERTS_PER_GROUP
    el = jnp.where((lane >= lo) & (lane < lo + EXPERTS_PER_GROUP), logits, NEG)
    m1 = jnp.max(el, axis=-1, keepdims=True)
    i1 = first_argmax(el, m1)
    el2 = jnp.where(lane == i1, NEG, el)
    m2 = jnp.max(el2, axis=-1, keepdims=True)
    i2 = first_argmax(el2, m2)
    r = jnp.exp(m2 - m1)
    w1 = 1.0 / (1.0 + r)
    w2 = r / (1.0 + r)
    out = jnp.where(lane == 0, i1 - N_GROUPS, 0.0)
    out = jnp.where(lane == 1, i2 - N_GROUPS, out)
    out = jnp.where(lane == 2, grp_w * w1, out)
    out = jnp.where(lane == 3, grp_w * w2, out)
    r_ref[...] = out


def _norm_router(x1, norm_w, w_group, w_router):
    t, d = x1.shape
    tm = min(256, t)
    wr = jnp.concatenate([w_group, w_router], axis=1).astype(F32)
    wr = jnp.pad(wr, ((0, 0), (0, HEAD_DIM - wr.shape[1])))
    return pl.pallas_call(
        _router_body,
        out_shape=(jax.ShapeDtypeStruct((t, d), F32), jax.ShapeDtypeStruct((t, HEAD_DIM), F32)),
        grid=(t // tm,),
        in_specs=[
            pl.BlockSpec((tm, d), lambda i: (i, 0)),
            pl.BlockSpec((1, d), lambda i: (0, 0)),
            pl.BlockSpec((d, HEAD_DIM), lambda i: (0, 0)),
        ],
        out_specs=(pl.BlockSpec((tm, d), lambda i: (i, 0)), pl.BlockSpec((tm, HEAD_DIM), lambda i: (i, 0))),
        compiler_params=_cparams(("parallel",)),
        name="norm_router",
    )(x1, norm_w.astype(F32).reshape(1, d), wr)


def _row_gather_start(src_hbm, idx_ref, dst, sem, row0, n_rows):
    def body(r, carry):
        rr = row0 + r
        pltpu.make_async_copy(src_hbm.at[pl.ds(idx_ref[0, rr], 1)], dst.at[pl.ds(rr, 1)], sem).start()
        return carry

    lax.fori_loop(0, n_rows, body, 0, unroll=8)


def _row_gather_wait(dst, sem):
    pltpu.make_async_copy(dst, dst, sem).wait()


MOE_BM = 256


def _ffn_body(nblk_ref, blke_ref, tok_ref, tokn_ref, x_hbm, wg_ref, wu_ref, wd_ref, y_ref,
              xbuf, wg_bf, wu_bf, wd_bf, sem):
    i = pl.program_id(0)
    nblk = nblk_ref[0]
    slot = i % 2

    @pl.when((i == 0) & (nblk > 0))
    def _():
        _row_gather_start(x_hbm, tok_ref, xbuf.at[0], sem.at[0], 0, MOE_BM)

    @pl.when(i + 1 < nblk)
    def _():
        _row_gather_start(x_hbm, tokn_ref, xbuf.at[1 - slot], sem.at[1 - slot], 0, MOE_BM)

    new_expert = (i == 0) | (blke_ref[i] != blke_ref[jnp.maximum(i - 1, 0)])

    @pl.when((i < nblk) & new_expert)
    def _():
        wg_bf[...] = wg_ref[...].astype(BF16)
        wu_bf[...] = wu_ref[...].astype(BF16)
        wd_bf[...] = wd_ref[...].astype(BF16)

    @pl.when(i < nblk)
    def _():
        _row_gather_wait(xbuf.at[slot], sem.at[slot])
        x = xbuf[slot].astype(BF16)
        g = jnp.dot(x, wg_bf[...], preferred_element_type=F32)
        u = jnp.dot(x, wu_bf[...], preferred_element_type=F32)
        hid = (g * jax.nn.sigmoid(g) * u).astype(BF16)
        y_ref[...] = jnp.dot(hid, wd_bf[...], preferred_element_type=F32)

    @pl.when(i >= nblk)
    def _():
        y_ref[...] = jnp.zeros_like(y_ref)


def _expert_ffn(h2, w_gate, w_up, w_down, nblk, blk_e, slot_tok):
    t, d = h2.shape
    n_blocks = blk_e.shape[0]
    tok_arr = slot_tok.reshape(n_blocks, 1, MOE_BM)
    smem_blk = functools.partial(pl.BlockSpec, (None, 1, MOE_BM), memory_space=pltpu.SMEM)
    single = pl.Buffered(1)
    grid_spec = pltpu.PrefetchScalarGridSpec(
        num_scalar_prefetch=2,
        grid=(n_blocks,),
        in_specs=[
            smem_blk(lambda i, nb, be: (i, 0, 0)),
            smem_blk(lambda i, nb, be: (jnp.minimum(i + 1, n_blocks - 1), 0, 0)),
            pl.BlockSpec(memory_space=pl.ANY),
            pl.BlockSpec((None, d, D_EXPERT), lambda i, nb, be: (be[i], 0, 0), pipeline_mode=single),
            pl.BlockSpec((None, d, D_EXPERT), lambda i, nb, be: (be[i], 0, 0), pipeline_mode=single),
            pl.BlockSpec((None, D_EXPERT, d), lambda i, nb, be: (be[i], 0, 0), pipeline_mode=single),
        ],
        out_specs=pl.BlockSpec((MOE_BM, d), lambda i, nb, be: (i, 0)),
        scratch_shapes=[
            pltpu.VMEM((2, MOE_BM, d), F32),
            pltpu.VMEM((d, D_EXPERT), BF16),
            pltpu.VMEM((d, D_EXPERT), BF16),
            pltpu.VMEM((D_EXPERT, d), BF16),
            pltpu.SemaphoreType.DMA((2,)),
        ],
    )
    return pl.pallas_call(
        _ffn_body,
        out_shape=jax.ShapeDtypeStruct((n_blocks * MOE_BM, d), F32),
        grid_spec=grid_spec,
        compiler_params=_cparams(("arbitrary",)),
        name="expert_ffn",
    )(nblk, blk_e, tok_arr, tok_arr, h2, w_gate, w_up, w_down)


def _combine_body(d0_ref, d1_ref, d0n_ref, d1n_ref, x_ref, r_ref, y_hbm, o_ref, ybuf, sem):
    i = pl.program_id(0)
    n = pl.num_programs(0)
    slot = i % 2

    @pl.when(i == 0)
    def _():
        _row_gather_start(y_hbm, d0_ref, ybuf.at[0, 0], sem.at[0], 0, COMBINE_TM)
        _row_gather_start(y_hbm, d1_ref, ybuf.at[0, 1], sem.at[0], 0, COMBINE_TM)

    @pl.when(i + 1 < n)
    def _():
        _row_gather_start(y_hbm, d0n_ref, ybuf.at[1 - slot, 0], sem.at[1 - slot], 0, COMBINE_TM)
        _row_gather_start(y_hbm, d1n_ref, ybuf.at[1 - slot, 1], sem.at[1 - slot], 0, COMBINE_TM)

    _row_gather_wait(ybuf.at[slot], sem.at[slot])
    route = r_ref[...]
    o_ref[...] = x_ref[...] + route[:, 2:3] * ybuf[slot, 0] + route[:, 3:4] * ybuf[slot, 1]


def _combine(x1, route, y, dest0, dest1):
    t, d = x1.shape
    tm = min(COMBINE_TM, t)
    assert tm == COMBINE_TM
    n = t // tm
    d0 = dest0.reshape(n, 1, tm)
    d1 = dest1.reshape(n, 1, tm)
    smem_blk = functools.partial(pl.BlockSpec, (None, 1, tm), memory_space=pltpu.SMEM)
    nxt = lambda i: (jnp.minimum(i + 1, n - 1), 0, 0)
    return pl.pallas_call(
        _combine_body,
        out_shape=jax.ShapeDtypeStruct((t, d), F32),
        grid=(n,),
        in_specs=[
            smem_blk(lambda i: (i, 0, 0)),
            smem_blk(lambda i: (i, 0, 0)),
            smem_blk(nxt),
            smem_blk(nxt),
            pl.BlockSpec((tm, d), lambda i: (i, 0)),
            pl.BlockSpec((tm, HEAD_DIM), lambda i: (i, 0)),
            pl.BlockSpec(memory_space=pl.ANY),
        ],
        out_specs=pl.BlockSpec((tm, d), lambda i: (i, 0)),
        scratch_shapes=[pltpu.VMEM((2, 2, tm, d), F32), pltpu.SemaphoreType.DMA((2,))],
        compiler_params=_cparams(("arbitrary",)),
        name="moe_combine",
    )(d0, d1, d0, d1, x1, route, y)


def _dispatch_plan(route, t):
    expert = route[:, 0:2].astype(jnp.int32)
    n_assign = 2 * t
    flat_e = expert.reshape(n_assign)
    order = jnp.argsort(flat_e).astype(jnp.int32)
    e_s = flat_e[order]
    below = jnp.sum(flat_e[None, :] < jnp.arange(N_EXPERTS + 1, dtype=jnp.int32)[:, None], axis=1)
    start = below[:-1].astype(jnp.int32)
    counts = (below[1:] - below[:-1]).astype(jnp.int32)
    padded = (counts + MOE_BM - 1) // MOE_BM * MOE_BM
    pad_end = jnp.cumsum(padded)
    pad_start = pad_end - padded
    dest_sorted = pad_start[e_s] + (jnp.arange(n_assign, dtype=jnp.int32) - start[e_s])
    _, dest = lax.sort_key_val(order, dest_sorted)
    dest = dest.reshape(t, 2)
    n_blocks = n_assign // MOE_BM + N_EXPERTS
    nblk = (pad_end[-1] // MOE_BM).astype(jnp.int32).reshape(1)
    blk_start = jnp.arange(n_blocks, dtype=jnp.int32) * MOE_BM
    blk_e = jnp.minimum(
        jnp.sum(pad_end[None, :] <= blk_start[:, None], axis=1), N_EXPERTS - 1).astype(jnp.int32)
    slot_e = jnp.repeat(blk_e, MOE_BM)
    off = jnp.arange(n_blocks * MOE_BM, dtype=jnp.int32) - pad_start[slot_e]
    src = jnp.clip(start[slot_e] + off, 0, n_assign - 1)
    slot_tok = jnp.where(off < counts[slot_e], order[src] // 2, 0).astype(jnp.int32)
    last_e = blk_e[jnp.maximum(nblk[0] - 1, 0)]
    blk_e = jnp.where(jnp.arange(n_blocks) < nblk[0], blk_e, last_e)
    return nblk, blk_e, slot_tok, dest[:, 0], dest[:, 1]


def _layer(x, norm1_w, w_in, q_norm_w, k_norm_w, rel_bias, conv_w, a_log, dt_bias,
           o_norm_w, w_out, norm2_w, w_group, w_router, w_gate, w_up, w_down):
    bsz, seq, d = x.shape
    t = bsz * seq
    assert seq % SEQ_TILE == 0 and d % HEAD_DIM == 0 and t % COMBINE_TM == 0
    assert w_in.shape[1] == MAIN_COLS + 2 * N_HEADS
    x2d = x.reshape(t, d)

    w_in_t = jnp.swapaxes(w_in, 0, 1)
    h1 = _rmsnorm(x2d, norm1_w.astype(F32), BF16)
    p3 = _inproj(h1, w_in_t)
    gcol, grow = _gates(h1, w_in_t[MAIN_COLS:], a_log, dt_bias)

    ya = _band_attention(p3, _attn_bias_tile(rel_bias), q_norm_w, k_norm_w, bsz, seq)
    yb = _gated_delta(p3, conv_w, gcol, grow, o_norm_w, bsz, seq)
    x1 = _outproj(ya, yb, w_out, x2d)

    h2, route = _norm_router(x1, norm2_w, w_group, w_router)
    nblk, blk_e, slot_tok, dest0, dest1 = _dispatch_plan(route, t)
    y = _expert_ffn(h2, w_gate, w_up, w_down, nblk, blk_e, slot_tok)
    out = _combine(x1, route, y, dest0, dest1)
    return out.reshape(bsz, seq, d)


def kernel(x, norm1_w, w_in, q_norm_w, k_norm_w, rel_bias, conv_w, a_log, dt_bias, o_norm_w, w_out,
           norm2_w, w_group, w_router, w_gate, w_up, w_down):
    for l in range(norm1_w.shape[0]):
        x = _layer(x, norm1_w[l], w_in[l], q_norm_w[l], k_norm_w[l], rel_bias[l], conv_w[l], a_log[l],
                   dt_bias[l], o_norm_w[l], w_out[l], norm2_w[l], w_group[l], w_router[l], w_gate[l],
                   w_up[l], w_down[l])
    return x
```

```python
import functools

import jax
import jax.numpy as jnp
import numpy as np
from jax import lax
from jax.experimental import pallas as pl
from jax.experimental.pallas import tpu as pltpu

F32 = jnp.float32
BF16 = jnp.bfloat16

CHUNK = 64
HEAD_DIM = 128
N_HEADS = 16
WIDTH = N_HEADS * HEAD_DIM
LEFT_CHUNKS = 8
REL_CLIP = 256
CONV_K = 4
N_GROUPS = 8
EXPERTS_PER_GROUP = 8
N_EXPERTS = N_GROUPS * EXPERTS_PER_GROUP
D_EXPERT = 512
EPS = 1e-6
NEG = -1e30

BLK_ATT_Q, BLK_ATT_K, BLK_ATT_V = 0, 16, 32
BLK_DN_Q, BLK_DN_K, BLK_DN_V, BLK_DN_Z = 48, 64, 80, 96
N_MAIN_BLKS = 112
MAIN_COLS = N_MAIN_BLKS * HEAD_DIM

SEQ_TILE = 512
COMBINE_TM = 256

V7X_VMEM_LIMIT = 56 * 1024 * 1024


def _cparams(sem, vmem=V7X_VMEM_LIMIT):
    return pltpu.CompilerParams(dimension_semantics=sem, vmem_limit_bytes=vmem)


def _dot_nt(a, b):
    return lax.dot_general(a, b, (((1,), (1,)), ((), ())), preferred_element_type=F32)


def _rmsnorm_body(x_ref, w_ref, o_ref):
    x = x_ref[...]
    ms = jnp.mean(x * x, axis=-1, keepdims=True)
    o_ref[...] = (x * lax.rsqrt(ms + EPS) * w_ref[...]).astype(o_ref.dtype)


def _rmsnorm(x2d, w, out_dtype):
    t, d = x2d.shape
    tm = min(256, t)
    return pl.pallas_call(
        _rmsnorm_body,
        out_shape=jax.ShapeDtypeStruct((t, d), out_dtype),
        grid=(t // tm,),
        in_specs=[pl.BlockSpec((tm, d), lambda i: (i, 0)), pl.BlockSpec((1, d), lambda i: (0, 0))],
        out_specs=pl.BlockSpec((tm, d), lambda i: (i, 0)),
        compiler_params=_cparams(("parallel",)),
        name="rmsnorm",
    )(x2d, w.reshape(1, d))


def _inproj_body(h_ref, wt_ref, o_ref):
    acc = _dot_nt(h_ref[...], wt_ref[...].astype(BF16))
    for c in range(o_ref.shape[0]):
        o_ref[c] = acc[:, c * HEAD_DIM:(c + 1) * HEAD_DIM].astype(o_ref.dtype)


def _inproj(h, w_in_t):
    t, d = h.shape
    tm = min(1024, t)
    tn = 512
    nb = tn // HEAD_DIM
    return pl.pallas_call(
        _inproj_body,
        out_shape=jax.ShapeDtypeStruct((N_MAIN_BLKS, t, HEAD_DIM), BF16),
        grid=(t // tm, MAIN_COLS // tn),
        in_specs=[pl.BlockSpec((tm, d), lambda i, j: (i, 0)), pl.BlockSpec((tn, d), lambda i, j: (j, 0))],
        out_specs=pl.BlockSpec((nb, tm, HEAD_DIM), lambda i, j: (j, i, 0)),
        compiler_params=_cparams(("parallel", "arbitrary")),
        name="inproj",
    )(h, w_in_t)


def _chunk_cumsum_rows(x):
    row = lax.broadcasted_iota(jnp.int32, x.shape, 0) % CHUNK
    shift = 1
    while shift < CHUNK:
        x = x + jnp.where(row >= shift, pltpu.roll(x, shift, 0), 0.0)
        shift *= 2
    return x


def _gates_body(h_ref, wt_ref, a_ref, dtb_ref, col_ref, row_ref):
    p = _dot_nt(h_ref[...], wt_ref[...].astype(BF16))
    lane = lax.broadcasted_iota(jnp.int32, p.shape, 1)
    beta = jax.nn.sigmoid(p)
    z = p + dtb_ref[...]
    softplus = jnp.maximum(z, 0.0) + jnp.log1p(jnp.exp(-jnp.abs(z)))
    g = -a_ref[...] * softplus
    out = jnp.where(lane < N_HEADS, beta, _chunk_cumsum_rows(g))
    col_ref[...] = out
    row_ref[...] = out.T[N_HEADS:2 * N_HEADS, :]


def _gates(h, w_small_t, a_log, dt_bias):
    t, d = h.shape
    tm = min(1024, t)
    pad = HEAD_DIM - 2 * N_HEADS
    w = jnp.pad(w_small_t, ((0, pad), (0, 0)))
    a = jnp.pad(jnp.exp(a_log.astype(F32)), (N_HEADS, pad)).reshape(1, HEAD_DIM)
    dtb = jnp.pad(dt_bias.astype(F32), (N_HEADS, pad)).reshape(1, HEAD_DIM)
    return pl.pallas_call(
        _gates_body,
        out_shape=(jax.ShapeDtypeStruct((t, HEAD_DIM), F32), jax.ShapeDtypeStruct((N_HEADS, t), F32)),
        grid=(t // tm,),
        in_specs=[
            pl.BlockSpec((tm, d), lambda i: (i, 0)),
            pl.BlockSpec((HEAD_DIM, d), lambda i: (0, 0)),
            pl.BlockSpec((1, HEAD_DIM), lambda i: (0, 0)),
            pl.BlockSpec((1, HEAD_DIM), lambda i: (0, 0)),
        ],
        out_specs=(pl.BlockSpec((tm, HEAD_DIM), lambda i: (i, 0)), pl.BlockSpec((N_HEADS, tm), lambda i: (0, i))),
        compiler_params=_cparams(("parallel",)),
        name="gates",
    )(h, w, a, dtb)


ATT_SUB = 128
ATT_WIN = ATT_SUB + LEFT_CHUNKS * CHUNK


def _attn_bias_tile(rel_bias):
    period = ATT_SUB + ATT_WIN
    offs = np.arange(period)
    key_minus_query = np.where(offs < ATT_WIN, offs, offs - period)
    rel = LEFT_CHUNKS * CHUNK - key_minus_query
    diag_vals = rel_bias[:, np.clip(rel, -REL_CLIP, REL_CLIP) + REL_CLIP].astype(F32)
    flat = jnp.tile(diag_vals, (1, ATT_SUB))[:, :ATT_SUB * (period - 1)]
    bias = flat.reshape(-1, ATT_SUB, period - 1)[:, :, :ATT_WIN]
    qc = np.arange(ATT_SUB)[:, None] // CHUNK
    kc = np.arange(ATT_WIN)[None, :] // CHUNK
    allowed = (kc >= qc) & (kc <= qc + LEFT_CHUNKS)
    return jnp.where(allowed[None], bias, NEG)


def _head_rmsnorm(x, w):
    xf = x.astype(F32)
    return xf * lax.rsqrt(jnp.mean(xf * xf, axis=-1, keepdims=True) + EPS) * w


def _attn_body(q_ref, kp_ref, kc_ref, vp_ref, vc_ref, bias_ref, qw_ref, kw_ref, o_ref):
    t = pl.program_id(1)
    q = _head_rmsnorm(q_ref[...], qw_ref[...]).astype(BF16)
    k = jnp.concatenate(
        [_head_rmsnorm(kp_ref[...], kw_ref[...]), _head_rmsnorm(kc_ref[...], kw_ref[...])], axis=0
    ).astype(BF16)
    v = jnp.concatenate([vp_ref[...], vc_ref[...]], axis=0)
    n_sub = SEQ_TILE // ATT_SUB
    qb = q.reshape(n_sub, ATT_SUB, HEAD_DIM)
    kb = jnp.stack([k[p * ATT_SUB:p * ATT_SUB + ATT_WIN] for p in range(n_sub)], axis=0)
    vb = jnp.stack([v[p * ATT_SUB:p * ATT_SUB + ATT_WIN] for p in range(n_sub)], axis=0)
    s = jnp.einsum("pqd,pkd->pqk", qb, kb, preferred_element_type=F32) + bias_ref[...][None]
    col = lax.broadcasted_iota(jnp.int32, (n_sub, ATT_SUB, ATT_WIN), 2)
    sub = lax.broadcasted_iota(jnp.int32, (n_sub, ATT_SUB, ATT_WIN), 0)
    first_valid = jnp.where(t > 0, 0, SEQ_TILE) - sub * jnp.where(t > 0, 0, ATT_SUB)
    s = jnp.where(col >= first_valid, s, NEG)
    m = jnp.max(s, axis=-1, keepdims=True)
    e = jnp.exp(s - m)
    l = jnp.sum(e, axis=-1, keepdims=True)
    o = jnp.einsum("pqk,pkd->pqd", e.astype(BF16), vb, preferred_element_type=F32) / l
    o_ref[...] = o.reshape(SEQ_TILE, HEAD_DIM).astype(o_ref.dtype)


def _band_attention(p3, bias_tile, q_norm_w, k_norm_w, bsz, seq):
    nt = seq // SEQ_TILE
    qw = (q_norm_w.astype(F32) * (HEAD_DIM ** -0.5)).reshape(1, HEAD_DIM)
    kw = k_norm_w.astype(F32).reshape(1, HEAD_DIM)

    def cur(blk):
        return lambda bh, t: (blk + bh % N_HEADS, (bh // N_HEADS) * nt + t, 0)

    def prev(blk):
        return lambda bh, t: (blk + bh % N_HEADS, (bh // N_HEADS) * nt + jnp.maximum(t - 1, 0), 0)

    tile = (None, SEQ_TILE, HEAD_DIM)
    return pl.pallas_call(
        _attn_body,
        out_shape=jax.ShapeDtypeStruct((bsz * seq, WIDTH), BF16),
        grid=(bsz * N_HEADS, nt),
        in_specs=[
            pl.BlockSpec(tile, cur(BLK_ATT_Q)),
            pl.BlockSpec(tile, prev(BLK_ATT_K)),
            pl.BlockSpec(tile, cur(BLK_ATT_K)),
            pl.BlockSpec(tile, prev(BLK_ATT_V)),
            pl.BlockSpec(tile, cur(BLK_ATT_V)),
            pl.BlockSpec((None, ATT_SUB, ATT_WIN), lambda bh, t: (bh % N_HEADS, 0, 0)),
            pl.BlockSpec((1, HEAD_DIM), lambda bh, t: (0, 0)),
            pl.BlockSpec((1, HEAD_DIM), lambda bh, t: (0, 0)),
        ],
        out_specs=pl.BlockSpec((SEQ_TILE, HEAD_DIM), lambda bh, t: ((bh // N_HEADS) * nt + t, bh % N_HEADS)),
        compiler_params=_cparams(("parallel", "arbitrary")),
        name="band_attention",
    )(p3, p3, p3, p3, p3, bias_tile, qw, kw)


CONV_HALO = 16


def _short_conv_silu(cur, halo, w, t):
    cur = cur.astype(F32)
    halo = jnp.where(t > 0, halo.astype(F32), 0.0)
    cat = jnp.concatenate([halo, cur], axis=0)
    y = cur * w[CONV_K - 1:CONV_K]
    for i in range(CONV_K - 1):
        shift = CONV_K - 1 - i
        y = y + pltpu.roll(cat, shift, 0)[CONV_HALO:] * w[i:i + 1]
    return y * jax.nn.sigmoid(y)


def _l2norm(x):
    return x * lax.rsqrt(jnp.sum(x * x, axis=-1, keepdims=True) + EPS)


def _bdot(a, b):
    return jnp.dot(a.astype(BF16), b.astype(BF16), preferred_element_type=F32)


def _bdot_nt(a, b):
    return lax.dot_general(a.astype(BF16), b.astype(BF16), (((1,), (1,)), ((), ())), preferred_element_type=F32)


GDN_HB = 8


def _bmm(a, b):
    return jnp.einsum("gmk,gkn->gmn", a.astype(BF16), b.astype(BF16), preferred_element_type=F32)


def _bmm_nt(a, b):
    return jnp.einsum("gmk,gnk->gmn", a.astype(BF16), b.astype(BF16), preferred_element_type=F32)


def _gdn_chunk_terms(q, k, v, beta, gc, grow):
    ri = lax.broadcasted_iota(jnp.int32, (1, CHUNK, CHUNK), 1)
    ci = lax.broadcasted_iota(jnp.int32, (1, CHUNK, CHUNK), 2)
    gamma = jnp.exp(jnp.where(ri >= ci, gc - grow, NEG))
    kb = k * beta
    kqk = _bmm_nt(jnp.concatenate([kb, q], axis=1), k)
    m = jnp.where(ri > ci, kqk[:, :CHUNK] * gamma, 0.0)
    att = kqk[:, CHUNK:] * gamma
    inv = (ri == ci).astype(F32) - m
    pw = _bmm(m, m)
    for _ in range(4):
        both = _bmm(jnp.concatenate([inv, pw], axis=1), pw)
        inv = inv + both[:, :CHUNK]
        pw = both[:, CHUNK:]
    inv = inv + _bmm(inv, pw)
    eg = jnp.exp(gc)
    g_last = gc[:, CHUNK - 1:CHUNK]
    sol = _bmm(inv, jnp.concatenate([v * beta, kb * eg], axis=2))
    att_uw = _bmm(att, sol)
    k_dec = k * jnp.exp(g_last - gc)
    kd_uw = _bmm(jnp.swapaxes(k_dec, 1, 2), sol)
    lhs = jnp.concatenate([q * eg - att_uw[:, :, HEAD_DIM:], kd_uw[:, :, HEAD_DIM:]], axis=1)
    return lhs, att_uw[:, :, :HEAD_DIM], kd_uw[:, :, :HEAD_DIM], jnp.exp(g_last)


def _gdn_body(q_ref, k_ref, v_ref, qh_ref, kh_ref, vh_ref, wq_ref, wk_ref, wv_ref, z_ref,
              gcol_ref, grow_ref, ow_ref, o_ref, state_ref):
    group = pl.program_id(0) % (N_HEADS // GDN_HB)
    t = pl.program_id(1)

    @pl.when(t == 0)
    def _():
        state_ref[...] = jnp.zeros_like(state_ref)

    gates = gcol_ref[...]
    lane = lax.broadcasted_iota(jnp.int32, gates.shape, 1)
    n_chunks = SEQ_TILE // CHUNK
    n_prob = GDN_HB * n_chunks

    qs, ks, vs, betas, gcs, grows = [], [], [], [], [], []
    for hh in range(GDN_HB):
        head = group * GDN_HB + hh
        qs.append(_l2norm(_short_conv_silu(q_ref[hh], qh_ref[hh], wq_ref[hh], t)) * (HEAD_DIM ** -0.5))
        ks.append(_l2norm(_short_conv_silu(k_ref[hh], kh_ref[hh], wk_ref[hh], t)))
        vs.append(_short_conv_silu(v_ref[hh], vh_ref[hh], wv_ref[hh], t))
        betas.append(jnp.sum(jnp.where(lane == head, gates, 0.0), axis=-1, keepdims=True))
        gcs.append(jnp.sum(jnp.where(lane == head + N_HEADS, gates, 0.0), axis=-1, keepdims=True))
        grow = grow_ref[hh]
        grows.extend(grow[:, c * CHUNK:(c + 1) * CHUNK] for c in range(n_chunks))

    def chunked(parts):
        return jnp.stack(parts, axis=0).reshape(n_prob, CHUNK, parts[0].shape[-1])

    lhs, o_intra, s_add, decay = _gdn_chunk_terms(
        chunked(qs), chunked(ks), chunked(vs), chunked(betas), chunked(gcs), jnp.stack(grows, axis=0))
    lhs = lhs.reshape(GDN_HB, n_chunks, CHUNK + HEAD_DIM, HEAD_DIM)
    o_intra = o_intra.reshape(GDN_HB, n_chunks, CHUNK, HEAD_DIM)
    s_add = s_add.reshape(GDN_HB, n_chunks, HEAD_DIM, HEAD_DIM)
    decay = decay.reshape(GDN_HB, n_chunks, 1, 1)

    state = state_ref[...]
    ow = ow_ref[...]
    for c in range(n_chunks):
        prod = _bmm(lhs[:, c], state)
        o_c = prod[:, :CHUNK] + o_intra[:, c]
        state = state * decay[:, c] - prod[:, CHUNK:] + s_add[:, c]
        o_n = o_c * lax.rsqrt(jnp.mean(o_c * o_c, axis=-1, keepdims=True) + EPS) * ow
        for hh in range(GDN_HB):
            zc = z_ref[hh, c * CHUNK:(c + 1) * CHUNK, :].astype(F32)
            o_ref[c * CHUNK:(c + 1) * CHUNK, hh * HEAD_DIM:(hh + 1) * HEAD_DIM] = (
                o_n[hh] * (zc * jax.nn.sigmoid(zc))).astype(o_ref.dtype)
    state_ref[...] = state


def _gated_delta(p3, conv_w, gcol, grow, o_norm_w, bsz, seq):
    nt = seq // SEQ_TILE
    halos_per_tile = SEQ_TILE // CONV_HALO
    n_groups = N_HEADS // GDN_HB
    cw = conv_w.astype(F32).reshape(CONV_K, 3 * N_HEADS, HEAD_DIM).transpose(1, 0, 2)
    grow3 = grow.reshape(N_HEADS, 1, bsz * seq)

    def cur(blk):
        return lambda bg, t: (blk // GDN_HB + bg % n_groups, (bg // n_groups) * nt + t, 0)

    def halo(blk):
        return lambda bg, t: (
            blk // GDN_HB + bg % n_groups, jnp.maximum(((bg // n_groups) * nt + t) * halos_per_tile - 1, 0), 0)

    def cwmap(off):
        return lambda bg, t: (off // GDN_HB + bg % n_groups, 0, 0)

    tile = (GDN_HB, SEQ_TILE, HEAD_DIM)
    htile = (GDN_HB, CONV_HALO, HEAD_DIM)
    wtile = (GDN_HB, CONV_K, HEAD_DIM)
    return pl.pallas_call(
        _gdn_body,
        out_shape=jax.ShapeDtypeStruct((bsz * seq, WIDTH), BF16),
        grid=(bsz * n_groups, nt),
        in_specs=[
            pl.BlockSpec(tile, cur(BLK_DN_Q)),
            pl.BlockSpec(tile, cur(BLK_DN_K)),
            pl.BlockSpec(tile, cur(BLK_DN_V)),
            pl.BlockSpec(htile, halo(BLK_DN_Q)),
            pl.BlockSpec(htile, halo(BLK_DN_K)),
            pl.BlockSpec(htile, halo(BLK_DN_V)),
            pl.BlockSpec(wtile, cwmap(0)),
            pl.BlockSpec(wtile, cwmap(N_HEADS)),
            pl.BlockSpec(wtile, cwmap(2 * N_HEADS)),
            pl.BlockSpec(tile, cur(BLK_DN_Z)),
            pl.BlockSpec((SEQ_TILE, HEAD_DIM), lambda bg, t: ((bg // n_groups) * nt + t, 0)),
            pl.BlockSpec((GDN_HB, 1, SEQ_TILE), lambda bg, t: (bg % n_groups, 0, (bg // n_groups) * nt + t)),
            pl.BlockSpec((1, HEAD_DIM), lambda bg, t: (0, 0)),
        ],
        out_specs=pl.BlockSpec(
            (SEQ_TILE, GDN_HB * HEAD_DIM), lambda bg, t: ((bg // n_groups) * nt + t, bg % n_groups)),
        scratch_shapes=[pltpu.VMEM((GDN_HB, HEAD_DIM, HEAD_DIM), F32)],
        compiler_params=_cparams(("parallel", "arbitrary")),
        name="gated_delta",
    )(p3, p3, p3, p3, p3, p3, cw, cw, cw, p3, gcol, grow3, o_norm_w.astype(F32).reshape(1, HEAD_DIM))


def _outproj_body(ya_ref, yb_ref, wa_ref, wb_ref, x_ref, o_ref):
    acc = jnp.dot(ya_ref[...], wa_ref[...].astype(BF16), preferred_element_type=F32)
    acc = acc + jnp.dot(yb_ref[...], wb_ref[...].astype(BF16), preferred_element_type=F32)
    o_ref[...] = x_ref[...] + acc


def _outproj(ya, yb, w_out, x2d):
    t, d = x2d.shape
    tm = min(1024, t)
    tn = min(512, d)
    return pl.pallas_call(
        _outproj_body,
        out_shape=jax.ShapeDtypeStruct((t, d), F32),
        grid=(t // tm, d // tn),
        in_specs=[
            pl.BlockSpec((tm, WIDTH), lambda i, j: (i, 0)),
            pl.BlockSpec((tm, WIDTH), lambda i, j: (i, 0)),
            pl.BlockSpec((WIDTH, tn), lambda i, j: (0, j)),
            pl.BlockSpec((WIDTH, tn), lambda i, j: (1, j)),
            pl.BlockSpec((tm, tn), lambda i, j: (i, j)),
        ],
        out_specs=pl.BlockSpec((tm, tn), lambda i, j: (i, j)),
        compiler_params=_cparams(("parallel", "arbitrary")),
        name="outproj",
    )(ya, yb, w_out, w_out, x2d)


def _router_body(x_ref, nw_ref, wr_ref, h_ref, r_ref):
    x = x_ref[...]
    h = x * lax.rsqrt(jnp.mean(x * x, axis=-1, keepdims=True) + EPS) * nw_ref[...]
    h_ref[...] = h
    w = wr_ref[...]
    h_hi, w_hi = h.astype(BF16), w.astype(BF16)
    h_lo = (h - h_hi.astype(F32)).astype(BF16)
    w_lo = (w - w_hi.astype(F32)).astype(BF16)
    logits = (jnp.dot(h_hi, w_hi, preferred_element_type=F32) + jnp.dot(h_hi, w_lo, preferred_element_type=F32)
              + jnp.dot(h_lo, w_hi, preferred_element_type=F32))
    lane = lax.broadcasted_iota(jnp.int32, logits.shape, 1).astype(F32)
    big = float(HEAD_DIM)

    def first_argmax(vals, vmax):
        return jnp.min(jnp.where(vals == vmax, lane, big), axis=-1, keepdims=True)

    gl = jnp.where(lane < N_GROUPS, logits, NEG)
    gmax = jnp.max(gl, axis=-1, keepdims=True)
    gidx = first_argmax(gl, gmax)
    grp_w = 1.0 / jnp.sum(jnp.exp(gl - gmax), axis=-1, keepdims=True)
    lo = N_GROUPS + gidx * EXPERTS_PER_GROUP
    el = jnp.where((lane >= lo) & (lane < lo + EXPERTS_PER_GROUP), logits, NEG)
    m1 = jnp.max(el, axis=-1, keepdims=True)
    i1 = first_argmax(el, m1)
    el2 = jnp.where(lane == i1, NEG, el)
    m2 = jnp.max(el2, axis=-1, keepdims=True)
    i2 = first_argmax(el2, m2)
    r = jnp.exp(m2 - m1)
    w1 = 1.0 / (1.0 + r)
    w2 = r / (1.0 + r)
    out = jnp.where(lane == 0, i1 - N_GROUPS, 0.0)
    out = jnp.where(lane == 1, i2 - N_GROUPS, out)
    out = jnp.where(lane == 2, grp_w * w1, out)
    out = jnp.where(lane == 3, grp_w * w2, out)
    r_ref[...] = out


def _norm_router(x1, norm_w, w_group, w_router):
    t, d = x1.shape
    tm = min(256, t)
    wr = jnp.concatenate([w_group, w_router], axis=1).astype(F32)
    wr = jnp.pad(wr, ((0, 0), (0, HEAD_DIM - wr.shape[1])))
    return pl.pallas_call(
        _router_body,
        out_shape=(jax.ShapeDtypeStruct((t, d), F32), jax.ShapeDtypeStruct((t, HEAD_DIM), F32)),
        grid=(t // tm,),
        in_specs=[
            pl.BlockSpec((tm, d), lambda i: (i, 0)),
            pl.BlockSpec((1, d), lambda i: (0, 0)),
            pl.BlockSpec((d, HEAD_DIM), lambda i: (0, 0)),
        ],
        out_specs=(pl.BlockSpec((tm, d), lambda i: (i, 0)), pl.BlockSpec((tm, HEAD_DIM), lambda i: (i, 0))),
        compiler_params=_cparams(("parallel",)),
        name="norm_router",
    )(x1, norm_w.astype(F32).reshape(1, d), wr)


def _row_gather_start(src_hbm, idx_ref, dst, sem, row0, n_rows):
    def body(r, carry):
        rr = row0 + r
        pltpu.make_async_copy(src_hbm.at[pl.ds(idx_ref[0, rr], 1)], dst.at[pl.ds(rr, 1)], sem).start()
        return carry

    lax.fori_loop(0, n_rows, body, 0, unroll=8)


def _row_gather_wait(dst, sem):
    pltpu.make_async_copy(dst, dst, sem).wait()


MOE_BM = 256
FFN_CAST_CHUNKS = 8


def _expert_weight_copies(e, wg_hbm, wu_hbm, wg_f32, wu_f32, wsem):
    return (
        pltpu.make_async_copy(wg_hbm.at[e], wg_f32, wsem.at[0]),
        pltpu.make_async_copy(wu_hbm.at[e], wu_f32, wsem.at[1]),
    )


def _ffn_body(nblk_ref, blke_ref, nxte_ref, tok_ref, tokn_ref, x_hbm, wg_hbm, wu_hbm, wd_f32, y_ref,
              xbuf, wg_f32, wu_f32, wg_bf, wu_bf, wd_bf, sem, wsem):
    i = pl.program_id(0)
    nblk = nblk_ref[0]
    slot = i % 2
    weight_copies = functools.partial(
        _expert_weight_copies, wg_hbm=wg_hbm, wu_hbm=wu_hbm, wg_f32=wg_f32, wu_f32=wu_f32, wsem=wsem)

    @pl.when((i == 0) & (nblk > 0))
    def _():
        _row_gather_start(x_hbm, tok_ref, xbuf.at[0], sem.at[0], 0, MOE_BM)
        for c in weight_copies(blke_ref[0]):
            c.start()

    @pl.when(i + 1 < nblk)
    def _():
        _row_gather_start(x_hbm, tokn_ref, xbuf.at[1 - slot], sem.at[1 - slot], 0, MOE_BM)

    new_expert = (i == 0) | (blke_ref[i] != blke_ref[jnp.maximum(i - 1, 0)])

    @pl.when((i < nblk) & new_expert)
    def _():
        for c in weight_copies(blke_ref[i]):
            c.wait()
        for src, dst in ((wg_f32, wg_bf), (wu_f32, wu_bf), (wd_f32, wd_bf)):
            rows = src.shape[0] // FFN_CAST_CHUNKS
            for c in range(FFN_CAST_CHUNKS):
                dst[c * rows:(c + 1) * rows, :] = src[c * rows:(c + 1) * rows, :].astype(BF16)

        @pl.when(nxte_ref[i] >= 0)
        def _():
            for c in weight_copies(nxte_ref[i]):
                c.start()

    @pl.when(i < nblk)
    def _():
        _row_gather_wait(xbuf.at[slot], sem.at[slot])
        x = xbuf[slot].astype(BF16)
        g = jnp.dot(x, wg_bf[...], preferred_element_type=F32)
        u = jnp.dot(x, wu_bf[...], preferred_element_type=F32)
        hid = (g * jax.nn.sigmoid(g) * u).astype(BF16)
        y_ref[...] = jnp.dot(hid, wd_bf[...], preferred_element_type=F32)

    @pl.when(i >= nblk)
    def _():
        y_ref[...] = jnp.zeros_like(y_ref)


def _expert_ffn(h2, w_gate, w_up, w_down, nblk, blk_e, nxt_e, slot_tok):
    t, d = h2.shape
    n_blocks = blk_e.shape[0]
    tok_arr = slot_tok.reshape(n_blocks, 1, MOE_BM)
    smem_blk = functools.partial(pl.BlockSpec, (None, 1, MOE_BM), memory_space=pltpu.SMEM)
    hbm = pl.BlockSpec(memory_space=pl.ANY)
    grid_spec = pltpu.PrefetchScalarGridSpec(
        num_scalar_prefetch=3,
        grid=(n_blocks,),
        in_specs=[
            smem_blk(lambda i, nb, be, ne: (i, 0, 0)),
            smem_blk(lambda i, nb, be, ne: (jnp.minimum(i + 1, n_blocks - 1), 0, 0)),
            hbm, hbm, hbm,
            pl.BlockSpec((None, D_EXPERT, d), lambda i, nb, be, ne: (be[i], 0, 0), pipeline_mode=pl.Buffered(1)),
        ],
        out_specs=pl.BlockSpec((MOE_BM, d), lambda i, nb, be, ne: (i, 0)),
        scratch_shapes=[
            pltpu.VMEM((2, MOE_BM, d), F32),
            pltpu.VMEM((d, D_EXPERT), F32),
            pltpu.VMEM((d, D_EXPERT), F32),
            pltpu.VMEM((d, D_EXPERT), BF16),
            pltpu.VMEM((d, D_EXPERT), BF16),
            pltpu.VMEM((D_EXPERT, d), BF16),
            pltpu.SemaphoreType.DMA((2,)),
            pltpu.SemaphoreType.DMA((2,)),
        ],
    )
    return pl.pallas_call(
        _ffn_body,
        out_shape=jax.ShapeDtypeStruct((n_blocks * MOE_BM, d), F32),
        grid_spec=grid_spec,
        compiler_params=_cparams(("arbitrary",)),
        name="expert_ffn",
    )(nblk, blk_e, nxt_e, tok_arr, tok_arr, h2, w_gate, w_up, w_down)


def _combine_body(d0_ref, d1_ref, d0n_ref, d1n_ref, x_ref, r_ref, y_hbm, o_ref, ybuf, sem):
    i = pl.program_id(0)
    n = pl.num_programs(0)
    slot = i % 2

    @pl.when(i == 0)
    def _():
        _row_gather_start(y_hbm, d0_ref, ybuf.at[0, 0], sem.at[0], 0, COMBINE_TM)
        _row_gather_start(y_hbm, d1_ref, ybuf.at[0, 1], sem.at[0], 0, COMBINE_TM)

    @pl.when(i + 1 < n)
    def _():
        _row_gather_start(y_hbm, d0n_ref, ybuf.at[1 - slot, 0], sem.at[1 - slot], 0, COMBINE_TM)
        _row_gather_start(y_hbm, d1n_ref, ybuf.at[1 - slot, 1], sem.at[1 - slot], 0, COMBINE_TM)

    _row_gather_wait(ybuf.at[slot], sem.at[slot])
    route = r_ref[...]
    o_ref[...] = x_ref[...] + route[:, 2:3] * ybuf[slot, 0] + route[:, 3:4] * ybuf[slot, 1]


def _combine(x1, route, y, dest0, dest1):
    t, d = x1.shape
    tm = min(COMBINE_TM, t)
    assert tm == COMBINE_TM
    n = t // tm
    d0 = dest0.reshape(n, 1, tm)
    d1 = dest1.reshape(n, 1, tm)
    smem_blk = functools.partial(pl.BlockSpec, (None, 1, tm), memory_space=pltpu.SMEM)
    nxt = lambda i: (jnp.minimum(i + 1, n - 1), 0, 0)
    return pl.pallas_call(
        _combine_body,
        out_shape=jax.ShapeDtypeStruct((t, d), F32),
        grid=(n,),
        in_specs=[
            smem_blk(lambda i: (i, 0, 0)),
            smem_blk(lambda i: (i, 0, 0)),
            smem_blk(nxt),
            smem_blk(nxt),
            pl.BlockSpec((tm, d), lambda i: (i, 0)),
            pl.BlockSpec((tm, HEAD_DIM), lambda i: (i, 0)),
            pl.BlockSpec(memory_space=pl.ANY),
        ],
        out_specs=pl.BlockSpec((tm, d), lambda i: (i, 0)),
        scratch_shapes=[pltpu.VMEM((2, 2, tm, d), F32), pltpu.SemaphoreType.DMA((2,))],
        compiler_params=_cparams(("arbitrary",)),
        name="moe_combine",
    )(d0, d1, d0, d1, x1, route, y)


def _dispatch_plan(route, t):
    expert = route[:, 0:2].astype(jnp.int32)
    n_assign = 2 * t
    flat_e = expert.reshape(n_assign)
    order = jnp.argsort(flat_e).astype(jnp.int32)
    e_s = flat_e[order]
    below = jnp.sum(flat_e[None, :] < jnp.arange(N_EXPERTS + 1, dtype=jnp.int32)[:, None], axis=1)
    start = below[:-1].astype(jnp.int32)
    counts = (below[1:] - below[:-1]).astype(jnp.int32)
    padded = (counts + MOE_BM - 1) // MOE_BM * MOE_BM
    pad_end = jnp.cumsum(padded)
    pad_start = pad_end - padded
    dest_sorted = pad_start[e_s] + (jnp.arange(n_assign, dtype=jnp.int32) - start[e_s])
    _, dest = lax.sort_key_val(order, dest_sorted)
    dest = dest.reshape(t, 2)
    n_blocks = n_assign // MOE_BM + N_EXPERTS
    nblk = (pad_end[-1] // MOE_BM).astype(jnp.int32).reshape(1)
    blk_start = jnp.arange(n_blocks, dtype=jnp.int32) * MOE_BM
    blk_e = jnp.minimum(
        jnp.sum(pad_end[None, :] <= blk_start[:, None], axis=1), N_EXPERTS - 1).astype(jnp.int32)
    slot_e = jnp.repeat(blk_e, MOE_BM)
    off = jnp.arange(n_blocks * MOE_BM, dtype=jnp.int32) - pad_start[slot_e]
    src = jnp.clip(start[slot_e] + off, 0, n_assign - 1)
    slot_tok = jnp.where(off < counts[slot_e], order[src] // 2, 0).astype(jnp.int32)
    run_end = pad_end[blk_e] // MOE_BM
    nxt_e = jnp.where(run_end < nblk[0], blk_e[jnp.minimum(run_end, n_blocks - 1)], -1).astype(jnp.int32)
    last_e = blk_e[jnp.maximum(nblk[0] - 1, 0)]
    blk_e = jnp.where(jnp.arange(n_blocks) < nblk[0], blk_e, last_e)
    return nblk, blk_e, nxt_e, slot_tok, dest[:, 0], dest[:, 1]


def _layer(x, norm1_w, w_in, q_norm_w, k_norm_w, rel_bias, conv_w, a_log, dt_bias,
           o_norm_w, w_out, norm2_w, w_group, w_router, w_gate, w_up, w_down):
    bsz, seq, d = x.shape
    t = bsz * seq
    assert seq % SEQ_TILE == 0 and d % HEAD_DIM == 0 and t % COMBINE_TM == 0
    assert w_in.shape[1] == MAIN_COLS + 2 * N_HEADS
    x2d = x.reshape(t, d)

    w_in_t = jnp.swapaxes(w_in, 0, 1)
    h1 = _rmsnorm(x2d, norm1_w.astype(F32), BF16)
    p3 = _inproj(h1, w_in_t)
    gcol, grow = _gates(h1, w_in_t[MAIN_COLS:], a_log, dt_bias)

    ya = _band_attention(p3, _attn_bias_tile(rel_bias), q_norm_w, k_norm_w, bsz, seq)
    yb = _gated_delta(p3, conv_w, gcol, grow, o_norm_w, bsz, seq)
    x1 = _outproj(ya, yb, w_out, x2d)

    h2, route = _norm_router(x1, norm2_w, w_group, w_router)
    nblk, blk_e, nxt_e, slot_tok, dest0, dest1 = _dispatch_plan(route, t)
    y = _expert_ffn(h2, w_gate, w_up, w_down, nblk, blk_e, nxt_e, slot_tok)
    out = _combine(x1, route, y, dest0, dest1)
    return out.reshape(bsz, seq, d)


def kernel(x, norm1_w, w_in, q_norm_w, k_norm_w, rel_bias, conv_w, a_log, dt_bias, o_norm_w, w_out,
           norm2_w, w_group, w_router, w_gate, w_up, w_down):
    for l in range(norm1_w.shape[0]):
        x = _layer(x, norm1_w[l], w_in[l], q_norm_w[l], k_norm_w[l], rel_bias[l], conv_w[l], a_log[l],
                   dt_bias[l], o_norm_w[l], w_out[l], norm2_w[l], w_group[l], w_router[l], w_gate[l],
                   w_up[l], w_down[l])
    return x
```

```python
import functools

import jax
import jax.numpy as jnp
import numpy as np
from jax import lax
from jax.experimental import pallas as pl
from jax.experimental.pallas import tpu as pltpu

F32 = jnp.float32
BF16 = jnp.bfloat16

CHUNK = 64
HEAD_DIM = 128
N_HEADS = 16
WIDTH = N_HEADS * HEAD_DIM
LEFT_CHUNKS = 8
REL_CLIP = 256
CONV_K = 4
N_GROUPS = 8
EXPERTS_PER_GROUP = 8
N_EXPERTS = N_GROUPS * EXPERTS_PER_GROUP
D_EXPERT = 512
EPS = 1e-6
NEG = -1e30

BLK_ATT_Q, BLK_ATT_K, BLK_ATT_V = 0, 16, 32
BLK_DN_Q, BLK_DN_K, BLK_DN_V, BLK_DN_Z = 48, 64, 80, 96
N_MAIN_BLKS = 112
MAIN_COLS = N_MAIN_BLKS * HEAD_DIM

SEQ_TILE = 512
COMBINE_TM = 256

V7X_VMEM_LIMIT = 56 * 1024 * 1024


def _cparams(sem, vmem=V7X_VMEM_LIMIT):
    return pltpu.CompilerParams(dimension_semantics=sem, vmem_limit_bytes=vmem)


def _dot_nt(a, b):
    return lax.dot_general(a, b, (((1,), (1,)), ((), ())), preferred_element_type=F32)


def _rmsnorm_body(x_ref, w_ref, o_ref):
    x = x_ref[...]
    ms = jnp.mean(x * x, axis=-1, keepdims=True)
    o_ref[...] = (x * lax.rsqrt(ms + EPS) * w_ref[...]).astype(o_ref.dtype)


def _rmsnorm(x2d, w, out_dtype):
    t, d = x2d.shape
    tm = min(256, t)
    return pl.pallas_call(
        _rmsnorm_body,
        out_shape=jax.ShapeDtypeStruct((t, d), out_dtype),
        grid=(t // tm,),
        in_specs=[pl.BlockSpec((tm, d), lambda i: (i, 0)), pl.BlockSpec((1, d), lambda i: (0, 0))],
        out_specs=pl.BlockSpec((tm, d), lambda i: (i, 0)),
        compiler_params=_cparams(("parallel",)),
        name="rmsnorm",
    )(x2d, w.reshape(1, d))


def _inproj_body(h_ref, wt_ref, o_ref):
    acc = _dot_nt(h_ref[...], wt_ref[...].astype(BF16))
    for c in range(o_ref.shape[0]):
        o_ref[c] = acc[:, c * HEAD_DIM:(c + 1) * HEAD_DIM].astype(o_ref.dtype)


def _inproj(h, w_in_t):
    t, d = h.shape
    tm = min(1024, t)
    tn = 512
    nb = tn // HEAD_DIM
    return pl.pallas_call(
        _inproj_body,
        out_shape=jax.ShapeDtypeStruct((N_MAIN_BLKS, t, HEAD_DIM), BF16),
        grid=(t // tm, MAIN_COLS // tn),
        in_specs=[pl.BlockSpec((tm, d), lambda i, j: (i, 0)), pl.BlockSpec((tn, d), lambda i, j: (j, 0))],
        out_specs=pl.BlockSpec((nb, tm, HEAD_DIM), lambda i, j: (j, i, 0)),
        compiler_params=_cparams(("parallel", "arbitrary")),
        name="inproj",
    )(h, w_in_t)


def _chunk_cumsum_rows(x):
    row = lax.broadcasted_iota(jnp.int32, x.shape, 0) % CHUNK
    shift = 1
    while shift < CHUNK:
        x = x + jnp.where(row >= shift, pltpu.roll(x, shift, 0), 0.0)
        shift *= 2
    return x


def _gates_body(h_ref, wt_ref, a_ref, dtb_ref, col_ref, row_ref):
    p = _dot_nt(h_ref[...], wt_ref[...].astype(BF16))
    lane = lax.broadcasted_iota(jnp.int32, p.shape, 1)
    beta = jax.nn.sigmoid(p)
    z = p + dtb_ref[...]
    softplus = jnp.maximum(z, 0.0) + jnp.log1p(jnp.exp(-jnp.abs(z)))
    g = -a_ref[...] * softplus
    out = jnp.where(lane < N_HEADS, beta, _chunk_cumsum_rows(g))
    col_ref[...] = out
    row_ref[...] = out.T[N_HEADS:2 * N_HEADS, :]


def _gates(h, w_small_t, a_log, dt_bias):
    t, d = h.shape
    tm = min(1024, t)
    pad = HEAD_DIM - 2 * N_HEADS
    w = jnp.pad(w_small_t, ((0, pad), (0, 0)))
    a = jnp.pad(jnp.exp(a_log.astype(F32)), (N_HEADS, pad)).reshape(1, HEAD_DIM)
    dtb = jnp.pad(dt_bias.astype(F32), (N_HEADS, pad)).reshape(1, HEAD_DIM)
    return pl.pallas_call(
        _gates_body,
        out_shape=(jax.ShapeDtypeStruct((t, HEAD_DIM), F32), jax.ShapeDtypeStruct((N_HEADS, t), F32)),
        grid=(t // tm,),
        in_specs=[
            pl.BlockSpec((tm, d), lambda i: (i, 0)),
            pl.BlockSpec((HEAD_DIM, d), lambda i: (0, 0)),
            pl.BlockSpec((1, HEAD_DIM), lambda i: (0, 0)),
            pl.BlockSpec((1, HEAD_DIM), lambda i: (0, 0)),
        ],
        out_specs=(pl.BlockSpec((tm, HEAD_DIM), lambda i: (i, 0)), pl.BlockSpec((N_HEADS, tm), lambda i: (0, i))),
        compiler_params=_cparams(("parallel",)),
        name="gates",
    )(h, w, a, dtb)


ATT_SUB = 128
ATT_WIN = ATT_SUB + LEFT_CHUNKS * CHUNK


def _attn_bias_tile(rel_bias):
    period = ATT_SUB + ATT_WIN
    offs = np.arange(period)
    key_minus_query = np.where(offs < ATT_WIN, offs, offs - period)
    rel = LEFT_CHUNKS * CHUNK - key_minus_query
    diag_vals = rel_bias[:, np.clip(rel, -REL_CLIP, REL_CLIP) + REL_CLIP].astype(F32)
    flat = jnp.tile(diag_vals, (1, ATT_SUB))[:, :ATT_SUB * (period - 1)]
    bias = flat.reshape(-1, ATT_SUB, period - 1)[:, :, :ATT_WIN]
    qc = np.arange(ATT_SUB)[:, None] // CHUNK
    kc = np.arange(ATT_WIN)[None, :] // CHUNK
    allowed = (kc >= qc) & (kc <= qc + LEFT_CHUNKS)
    return jnp.where(allowed[None], bias, NEG)


def _head_rmsnorm(x, w):
    xf = x.astype(F32)
    return xf * lax.rsqrt(jnp.mean(xf * xf, axis=-1, keepdims=True) + EPS) * w


def _attn_body(q_ref, kp_ref, kc_ref, vp_ref, vc_ref, bias_ref, qw_ref, kw_ref, o_ref):
    t = pl.program_id(1)
    q = _head_rmsnorm(q_ref[...], qw_ref[...]).astype(BF16)
    k = jnp.concatenate(
        [_head_rmsnorm(kp_ref[...], kw_ref[...]), _head_rmsnorm(kc_ref[...], kw_ref[...])], axis=0
    ).astype(BF16)
    v = jnp.concatenate([vp_ref[...], vc_ref[...]], axis=0)
    n_sub = SEQ_TILE // ATT_SUB
    qb = q.reshape(n_sub, ATT_SUB, HEAD_DIM)
    kb = jnp.stack([k[p * ATT_SUB:p * ATT_SUB + ATT_WIN] for p in range(n_sub)], axis=0)
    vb = jnp.stack([v[p * ATT_SUB:p * ATT_SUB + ATT_WIN] for p in range(n_sub)], axis=0)
    s = jnp.einsum("pqd,pkd->pqk", qb, kb, preferred_element_type=F32) + bias_ref[...][None]
    col = lax.broadcasted_iota(jnp.int32, (n_sub, ATT_SUB, ATT_WIN), 2)
    sub = lax.broadcasted_iota(jnp.int32, (n_sub, ATT_SUB, ATT_WIN), 0)
    first_valid = jnp.where(t > 0, 0, SEQ_TILE) - sub * jnp.where(t > 0, 0, ATT_SUB)
    s = jnp.where(col >= first_valid, s, NEG)
    m = jnp.max(s, axis=-1, keepdims=True)
    e = jnp.exp(s - m)
    l = jnp.sum(e, axis=-1, keepdims=True)
    o = jnp.einsum("pqk,pkd->pqd", e.astype(BF16), vb, preferred_element_type=F32) / l
    o_ref[...] = o.reshape(SEQ_TILE, HEAD_DIM).astype(o_ref.dtype)


def _band_attention(p3, bias_tile, q_norm_w, k_norm_w, bsz, seq):
    nt = seq // SEQ_TILE
    qw = (q_norm_w.astype(F32) * (HEAD_DIM ** -0.5)).reshape(1, HEAD_DIM)
    kw = k_norm_w.astype(F32).reshape(1, HEAD_DIM)

    def cur(blk):
        return lambda bh, t: (blk + bh % N_HEADS, (bh // N_HEADS) * nt + t, 0)

    def prev(blk):
        return lambda bh, t: (blk + bh % N_HEADS, (bh // N_HEADS) * nt + jnp.maximum(t - 1, 0), 0)

    tile = (None, SEQ_TILE, HEAD_DIM)
    return pl.pallas_call(
        _attn_body,
        out_shape=jax.ShapeDtypeStruct((bsz * seq, WIDTH), BF16),
        grid=(bsz * N_HEADS, nt),
        in_specs=[
            pl.BlockSpec(tile, cur(BLK_ATT_Q)),
            pl.BlockSpec(tile, prev(BLK_ATT_K)),
            pl.BlockSpec(tile, cur(BLK_ATT_K)),
            pl.BlockSpec(tile, prev(BLK_ATT_V)),
            pl.BlockSpec(tile, cur(BLK_ATT_V)),
            pl.BlockSpec((None, ATT_SUB, ATT_WIN), lambda bh, t: (bh % N_HEADS, 0, 0)),
            pl.BlockSpec((1, HEAD_DIM), lambda bh, t: (0, 0)),
            pl.BlockSpec((1, HEAD_DIM), lambda bh, t: (0, 0)),
        ],
        out_specs=pl.BlockSpec((SEQ_TILE, HEAD_DIM), lambda bh, t: ((bh // N_HEADS) * nt + t, bh % N_HEADS)),
        compiler_params=_cparams(("parallel", "arbitrary")),
        name="band_attention",
    )(p3, p3, p3, p3, p3, bias_tile, qw, kw)


CONV_HALO = 16


def _short_conv_silu(cur, halo, w, t, cat_ref):
    cur = cur.astype(F32)
    cat_ref[0:CONV_HALO, :] = jnp.where(t > 0, halo.astype(F32), 0.0)
    cat_ref[CONV_HALO:, :] = cur
    y = cur * w[CONV_K - 1:CONV_K]
    for i in range(CONV_K - 1):
        shift = CONV_K - 1 - i
        y = y + cat_ref[CONV_HALO - shift:CONV_HALO - shift + SEQ_TILE, :] * w[i:i + 1]
    return y * jax.nn.sigmoid(y)


def _l2norm(x):
    return x * lax.rsqrt(jnp.sum(x * x, axis=-1, keepdims=True) + EPS)


def _bdot(a, b):
    return jnp.dot(a.astype(BF16), b.astype(BF16), preferred_element_type=F32)


def _bdot_nt(a, b):
    return lax.dot_general(a.astype(BF16), b.astype(BF16), (((1,), (1,)), ((), ())), preferred_element_type=F32)


GDN_HB = 8


def _bmm(a, b):
    return jnp.einsum("gmk,gkn->gmn", a.astype(BF16), b.astype(BF16), preferred_element_type=F32)


def _bmm_nt(a, b):
    return jnp.einsum("gmk,gnk->gmn", a.astype(BF16), b.astype(BF16), preferred_element_type=F32)


def _gdn_chunk_terms(q, k, v, beta, gc, grow):
    ri = lax.broadcasted_iota(jnp.int32, (1, CHUNK, CHUNK), 1)
    ci = lax.broadcasted_iota(jnp.int32, (1, CHUNK, CHUNK), 2)
    gamma = jnp.exp(jnp.where(ri >= ci, gc - grow, NEG))
    kb = k * beta
    kqk = _bmm_nt(jnp.concatenate([kb, q], axis=1), k)
    m = jnp.where(ri > ci, kqk[:, :CHUNK] * gamma, 0.0)
    att = kqk[:, CHUNK:] * gamma
    inv = (ri == ci).astype(F32) - m
    pw = _bmm(m, m)
    for _ in range(4):
        both = _bmm(jnp.concatenate([inv, pw], axis=1), pw)
        inv = inv + both[:, :CHUNK]
        pw = both[:, CHUNK:]
    inv = inv + _bmm(inv, pw)
    eg = jnp.exp(gc)
    g_last = gc[:, CHUNK - 1:CHUNK]
    sol = _bmm(inv, jnp.concatenate([v * beta, kb * eg], axis=2))
    att_uw = _bmm(att, sol)
    k_dec = k * jnp.exp(g_last - gc)
    kd_uw = _bmm(jnp.swapaxes(k_dec, 1, 2), sol)
    lhs = jnp.concatenate([q * eg - att_uw[:, :, HEAD_DIM:], kd_uw[:, :, HEAD_DIM:]], axis=1)
    return lhs, att_uw[:, :, :HEAD_DIM], kd_uw[:, :, :HEAD_DIM], jnp.exp(g_last)


def _gdn_body(q_ref, k_ref, v_ref, qh_ref, kh_ref, vh_ref, wq_ref, wk_ref, wv_ref, z_ref,
              gcol_ref, grow_ref, ow_ref, o_ref, state_ref, cat_ref):
    group = pl.program_id(0) % (N_HEADS // GDN_HB)
    t = pl.program_id(1)

    @pl.when(t == 0)
    def _():
        state_ref[...] = jnp.zeros_like(state_ref)

    gates = gcol_ref[...]
    lane = lax.broadcasted_iota(jnp.int32, gates.shape, 1)
    n_chunks = SEQ_TILE // CHUNK
    n_prob = GDN_HB * n_chunks

    qs, ks, vs, betas, gcs, grows = [], [], [], [], [], []
    for hh in range(GDN_HB):
        head = group * GDN_HB + hh
        qs.append(_l2norm(_short_conv_silu(q_ref[hh], qh_ref[hh], wq_ref[hh], t, cat_ref.at[3 * hh]))
                  * (HEAD_DIM ** -0.5))
        ks.append(_l2norm(_short_conv_silu(k_ref[hh], kh_ref[hh], wk_ref[hh], t, cat_ref.at[3 * hh + 1])))
        vs.append(_short_conv_silu(v_ref[hh], vh_ref[hh], wv_ref[hh], t, cat_ref.at[3 * hh + 2]))
        betas.append(jnp.sum(jnp.where(lane == head, gates, 0.0), axis=-1, keepdims=True))
        gcs.append(jnp.sum(jnp.where(lane == head + N_HEADS, gates, 0.0), axis=-1, keepdims=True))
        grow = grow_ref[hh]
        grows.extend(grow[:, c * CHUNK:(c + 1) * CHUNK] for c in range(n_chunks))

    def chunked(parts):
        return jnp.stack(parts, axis=0).reshape(n_prob, CHUNK, parts[0].shape[-1])

    lhs, o_intra, s_add, decay = _gdn_chunk_terms(
        chunked(qs), chunked(ks), chunked(vs), chunked(betas), chunked(gcs), jnp.stack(grows, axis=0))
    lhs = lhs.reshape(GDN_HB, n_chunks, CHUNK + HEAD_DIM, HEAD_DIM)
    o_intra = o_intra.reshape(GDN_HB, n_chunks, CHUNK, HEAD_DIM)
    s_add = s_add.reshape(GDN_HB, n_chunks, HEAD_DIM, HEAD_DIM)
    decay = decay.reshape(GDN_HB, n_chunks, 1, 1)

    state = state_ref[...]
    ow = ow_ref[...]
    for c in range(n_chunks):
        prod = _bmm(lhs[:, c], state)
        o_c = prod[:, :CHUNK] + o_intra[:, c]
        state = state * decay[:, c] - prod[:, CHUNK:] + s_add[:, c]
        o_n = o_c * lax.rsqrt(jnp.mean(o_c * o_c, axis=-1, keepdims=True) + EPS) * ow
        for hh in range(GDN_HB):
            zc = z_ref[hh, c * CHUNK:(c + 1) * CHUNK, :].astype(F32)
            o_ref[c * CHUNK:(c + 1) * CHUNK, hh * HEAD_DIM:(hh + 1) * HEAD_DIM] = (
                o_n[hh] * (zc * jax.nn.sigmoid(zc))).astype(o_ref.dtype)
    state_ref[...] = state


def _gated_delta(p3, conv_w, gcol, grow, o_norm_w, bsz, seq):
    nt = seq // SEQ_TILE
    halos_per_tile = SEQ_TILE // CONV_HALO
    n_groups = N_HEADS // GDN_HB
    cw = conv_w.astype(F32).reshape(CONV_K, 3 * N_HEADS, HEAD_DIM).transpose(1, 0, 2)
    grow3 = grow.reshape(N_HEADS, 1, bsz * seq)

    def cur(blk):
        return lambda bg, t: (blk // GDN_HB + bg % n_groups, (bg // n_groups) * nt + t, 0)

    def halo(blk):
        return lambda bg, t: (
            blk // GDN_HB + bg % n_groups, jnp.maximum(((bg // n_groups) * nt + t) * halos_per_tile - 1, 0), 0)

    def cwmap(off):
        return lambda bg, t: (off // GDN_HB + bg % n_groups, 0, 0)

    tile = (GDN_HB, SEQ_TILE, HEAD_DIM)
    htile = (GDN_HB, CONV_HALO, HEAD_DIM)
    wtile = (GDN_HB, CONV_K, HEAD_DIM)
    return pl.pallas_call(
        _gdn_body,
        out_shape=jax.ShapeDtypeStruct((bsz * seq, WIDTH), BF16),
        grid=(bsz * n_groups, nt),
        in_specs=[
            pl.BlockSpec(tile, cur(BLK_DN_Q)),
            pl.BlockSpec(tile, cur(BLK_DN_K)),
            pl.BlockSpec(tile, cur(BLK_DN_V)),
            pl.BlockSpec(htile, halo(BLK_DN_Q)),
            pl.BlockSpec(htile, halo(BLK_DN_K)),
            pl.BlockSpec(htile, halo(BLK_DN_V)),
            pl.BlockSpec(wtile, cwmap(0)),
            pl.BlockSpec(wtile, cwmap(N_HEADS)),
            pl.BlockSpec(wtile, cwmap(2 * N_HEADS)),
            pl.BlockSpec(tile, cur(BLK_DN_Z)),
            pl.BlockSpec((SEQ_TILE, HEAD_DIM), lambda bg, t: ((bg // n_groups) * nt + t, 0)),
            pl.BlockSpec((GDN_HB, 1, SEQ_TILE), lambda bg, t: (bg % n_groups, 0, (bg // n_groups) * nt + t)),
            pl.BlockSpec((1, HEAD_DIM), lambda bg, t: (0, 0)),
        ],
        out_specs=pl.BlockSpec(
            (SEQ_TILE, GDN_HB * HEAD_DIM), lambda bg, t: ((bg // n_groups) * nt + t, bg % n_groups)),
        scratch_shapes=[
            pltpu.VMEM((GDN_HB, HEAD_DIM, HEAD_DIM), F32),
            pltpu.VMEM((3 * GDN_HB, CONV_HALO + SEQ_TILE, HEAD_DIM), F32),
        ],
        compiler_params=_cparams(("parallel", "arbitrary")),
        name="gated_delta",
    )(p3, p3, p3, p3, p3, p3, cw, cw, cw, p3, gcol, grow3, o_norm_w.astype(F32).reshape(1, HEAD_DIM))


def _outproj_body(ya_ref, yb_ref, wa_ref, wb_ref, x_ref, o_ref):
    acc = jnp.dot(ya_ref[...], wa_ref[...].astype(BF16), preferred_element_type=F32)
    acc = acc + jnp.dot(yb_ref[...], wb_ref[...].astype(BF16), preferred_element_type=F32)
    o_ref[...] = x_ref[...] + acc


def _outproj(ya, yb, w_out, x2d):
    t, d = x2d.shape
    tm = min(1024, t)
    tn = min(512, d)
    return pl.pallas_call(
        _outproj_body,
        out_shape=jax.ShapeDtypeStruct((t, d), F32),
        grid=(t // tm, d // tn),
        in_specs=[
            pl.BlockSpec((tm, WIDTH), lambda i, j: (i, 0)),
            pl.BlockSpec((tm, WIDTH), lambda i, j: (i, 0)),
            pl.BlockSpec((WIDTH, tn), lambda i, j: (0, j)),
            pl.BlockSpec((WIDTH, tn), lambda i, j: (1, j)),
            pl.BlockSpec((tm, tn), lambda i, j: (i, j)),
        ],
        out_specs=pl.BlockSpec((tm, tn), lambda i, j: (i, j)),
        compiler_params=_cparams(("parallel", "arbitrary")),
        name="outproj",
    )(ya, yb, w_out, w_out, x2d)


def _router_body(x_ref, nw_ref, wr_ref, h_ref, r_ref):
    x = x_ref[...]
    h = x * lax.rsqrt(jnp.mean(x * x, axis=-1, keepdims=True) + EPS) * nw_ref[...]
    h_ref[...] = h
    w = wr_ref[...]
    h_hi, w_hi = h.astype(BF16), w.astype(BF16)
    h_lo = (h - h_hi.astype(F32)).astype(BF16)
    w_lo = (w - w_hi.astype(F32)).astype(BF16)
    logits = (jnp.dot(h_hi, w_hi, preferred_element_type=F32) + jnp.dot(h_hi, w_lo, preferred_element_type=F32)
              + jnp.dot(h_lo, w_hi, preferred_element_type=F32))
    lane = lax.broadcasted_iota(jnp.int32, logits.shape, 1).astype(F32)
    big = float(HEAD_DIM)

    def first_argmax(vals, vmax):
        return jnp.min(jnp.where(vals == vmax, lane, big), axis=-1, keepdims=True)

    gl = jnp.where(lane < N_GROUPS, logits, NEG)
    gmax = jnp.max(gl, axis=-1, keepdims=True)
    gidx = first_argmax(gl, gmax)
    grp_w = 1.0 / jnp.sum(jnp.exp(gl - gmax), axis=-1, keepdims=True)
    lo = N_GROUPS + gidx * EXPERTS_PER_GROUP
    el = jnp.where((lane >= lo) & (lane < lo + EXPERTS_PER_GROUP), logits, NEG)
    m1 = jnp.max(el, axis=-1, keepdims=True)
    i1 = first_argmax(el, m1)
    el2 = jnp.where(lane == i1, NEG, el)
    m2 = jnp.max(el2, axis=-1, keepdims=True)
    i2 = first_argmax(el2, m2)
    r = jnp.exp(m2 - m1)
    w1 = 1.0 / (1.0 + r)
    w2 = r / (1.0 + r)
    out = jnp.where(lane == 0, i1 - N_GROUPS, 0.0)
    out = jnp.where(lane == 1, i2 - N_GROUPS, out)
    out = jnp.where(lane == 2, grp_w * w1, out)
    out = jnp.where(lane == 3, grp_w * w2, out)
    r_ref[...] = out


def _norm_router(x1, norm_w, w_group, w_router):
    t, d = x1.shape
    tm = min(256, t)
    wr = jnp.concatenate([w_group, w_router], axis=1).astype(F32)
    wr = jnp.pad(wr, ((0, 0), (0, HEAD_DIM - wr.shape[1])))
    return pl.pallas_call(
        _router_body,
        out_shape=(jax.ShapeDtypeStruct((t, d), F32), jax.ShapeDtypeStruct((t, HEAD_DIM), F32)),
        grid=(t // tm,),
        in_specs=[
            pl.BlockSpec((tm, d), lambda i: (i, 0)),
            pl.BlockSpec((1, d), lambda i: (0, 0)),
            pl.BlockSpec((d, HEAD_DIM), lambda i: (0, 0)),
        ],
        out_specs=(pl.BlockSpec((tm, d), lambda i: (i, 0)), pl.BlockSpec((tm, HEAD_DIM), lambda i: (i, 0))),
        compiler_params=_cparams(("parallel",)),
        name="norm_router",
    )(x1, norm_w.astype(F32).reshape(1, d), wr)


def _row_gather_start(src_hbm, idx_ref, dst, sem, row0, n_rows):
    def body(r, carry):
        rr = row0 + r
        pltpu.make_async_copy(src_hbm.at[pl.ds(idx_ref[0, rr], 1)], dst.at[pl.ds(rr, 1)], sem).start()
        return carry

    lax.fori_loop(0, n_rows, body, 0, unroll=8)


def _row_gather_wait(dst, sem):
    pltpu.make_async_copy(dst, dst, sem).wait()


MOE_BM = 256
FFN_CAST_CHUNKS = 8


def _expert_weight_copies(e, wg_hbm, wu_hbm, wg_f32, wu_f32, wsem):
    return (
        pltpu.make_async_copy(wg_hbm.at[e], wg_f32, wsem.at[0]),
        pltpu.make_async_copy(wu_hbm.at[e], wu_f32, wsem.at[1]),
    )


def _ffn_body(nblk_ref, blke_ref, nxte_ref, tok_ref, tokn_ref, x_hbm, wg_hbm, wu_hbm, wd_f32, y_ref,
              xbuf, wg_f32, wu_f32, wg_bf, wu_bf, wd_bf, sem, wsem):
    i = pl.program_id(0)
    nblk = nblk_ref[0]
    slot = i % 2
    weight_copies = functools.partial(
        _expert_weight_copies, wg_hbm=wg_hbm, wu_hbm=wu_hbm, wg_f32=wg_f32, wu_f32=wu_f32, wsem=wsem)

    @pl.when((i == 0) & (nblk > 0))
    def _():
        _row_gather_start(x_hbm, tok_ref, xbuf.at[0], sem.at[0], 0, MOE_BM)
        for c in weight_copies(blke_ref[0]):
            c.start()

    @pl.when(i + 1 < nblk)
    def _():
        _row_gather_start(x_hbm, tokn_ref, xbuf.at[1 - slot], sem.at[1 - slot], 0, MOE_BM)

    new_expert = (i == 0) | (blke_ref[i] != blke_ref[jnp.maximum(i - 1, 0)])

    @pl.when((i < nblk) & new_expert)
    def _():
        for c in weight_copies(blke_ref[i]):
            c.wait()
        for src, dst in ((wg_f32, wg_bf), (wu_f32, wu_bf), (wd_f32, wd_bf)):
            rows = src.shape[0] // FFN_CAST_CHUNKS
            for c in range(FFN_CAST_CHUNKS):
                dst[c * rows:(c + 1) * rows, :] = src[c * rows:(c + 1) * rows, :].astype(BF16)

        @pl.when(nxte_ref[i] >= 0)
        def _():
            for c in weight_copies(nxte_ref[i]):
                c.start()

    @pl.when(i < nblk)
    def _():
        _row_gather_wait(xbuf.at[slot], sem.at[slot])
        x = xbuf[slot].astype(BF16)
        g = jnp.dot(x, wg_bf[...], preferred_element_type=F32)
        u = jnp.dot(x, wu_bf[...], preferred_element_type=F32)
        hid = (g * jax.nn.sigmoid(g) * u).astype(BF16)
        y_ref[...] = jnp.dot(hid, wd_bf[...], preferred_element_type=F32)

    @pl.when(i >= nblk)
    def _():
        y_ref[...] = jnp.zeros_like(y_ref)


def _expert_ffn(h2, w_gate, w_up, w_down, nblk, blk_e, nxt_e, slot_tok):
    t, d = h2.shape
    n_blocks = blk_e.shape[0]
    tok_arr = slot_tok.reshape(n_blocks, 1, MOE_BM)
    smem_blk = functools.partial(pl.BlockSpec, (None, 1, MOE_BM), memory_space=pltpu.SMEM)
    hbm = pl.BlockSpec(memory_space=pl.ANY)
    grid_spec = pltpu.PrefetchScalarGridSpec(
        num_scalar_prefetch=3,
        grid=(n_blocks,),
        in_specs=[
            smem_blk(lambda i, nb, be, ne: (i, 0, 0)),
            smem_blk(lambda i, nb, be, ne: (jnp.minimum(i + 1, n_blocks - 1), 0, 0)),
            hbm, hbm, hbm,
            pl.BlockSpec((None, D_EXPERT, d), lambda i, nb, be, ne: (be[i], 0, 0), pipeline_mode=pl.Buffered(1)),
        ],
        out_specs=pl.BlockSpec((MOE_BM, d), lambda i, nb, be, ne: (i, 0)),
        scratch_shapes=[
            pltpu.VMEM((2, MOE_BM, d), F32),
            pltpu.VMEM((d, D_EXPERT), F32),
            pltpu.VMEM((d, D_EXPERT), F32),
            pltpu.VMEM((d, D_EXPERT), BF16),
            pltpu.VMEM((d, D_EXPERT), BF16),
            pltpu.VMEM((D_EXPERT, d), BF16),
            pltpu.SemaphoreType.DMA((2,)),
            pltpu.SemaphoreType.DMA((2,)),
        ],
    )
    return pl.pallas_call(
        _ffn_body,
        out_shape=jax.ShapeDtypeStruct((n_blocks * MOE_BM, d), F32),
        grid_spec=grid_spec,
        compiler_params=_cparams(("arbitrary",)),
        name="expert_ffn",
    )(nblk, blk_e, nxt_e, tok_arr, tok_arr, h2, w_gate, w_up, w_down)


def _combine_body(d0_ref, d1_ref, d0n_ref, d1n_ref, x_ref, r_ref, y_hbm, o_ref, ybuf, sem):
    i = pl.program_id(0)
    n = pl.num_programs(0)
    slot = i % 2

    @pl.when(i == 0)
    def _():
        _row_gather_start(y_hbm, d0_ref, ybuf.at[0, 0], sem.at[0], 0, COMBINE_TM)
        _row_gather_start(y_hbm, d1_ref, ybuf.at[0, 1], sem.at[0], 0, COMBINE_TM)

    @pl.when(i + 1 < n)
    def _():
        _row_gather_start(y_hbm, d0n_ref, ybuf.at[1 - slot, 0], sem.at[1 - slot], 0, COMBINE_TM)
        _row_gather_start(y_hbm, d1n_ref, ybuf.at[1 - slot, 1], sem.at[1 - slot], 0, COMBINE_TM)

    _row_gather_wait(ybuf.at[slot], sem.at[slot])
    route = r_ref[...]
    o_ref[...] = x_ref[...] + route[:, 2:3] * ybuf[slot, 0] + route[:, 3:4] * ybuf[slot, 1]


def _combine(x1, route, y, dest0, dest1):
    t, d = x1.shape
    tm = min(COMBINE_TM, t)
    assert tm == COMBINE_TM
    n = t // tm
    d0 = dest0.reshape(n, 1, tm)
    d1 = dest1.reshape(n, 1, tm)
    smem_blk = functools.partial(pl.BlockSpec, (None, 1, tm), memory_space=pltpu.SMEM)
    nxt = lambda i: (jnp.minimum(i + 1, n - 1), 0, 0)
    return pl.pallas_call(
        _combine_body,
        out_shape=jax.ShapeDtypeStruct((t, d), F32),
        grid=(n,),
        in_specs=[
            smem_blk(lambda i: (i, 0, 0)),
            smem_blk(lambda i: (i, 0, 0)),
            smem_blk(nxt),
            smem_blk(nxt),
            pl.BlockSpec((tm, d), lambda i: (i, 0)),
            pl.BlockSpec((tm, HEAD_DIM), lambda i: (i, 0)),
            pl.BlockSpec(memory_space=pl.ANY),
        ],
        out_specs=pl.BlockSpec((tm, d), lambda i: (i, 0)),
        scratch_shapes=[pltpu.VMEM((2, 2, tm, d), F32), pltpu.SemaphoreType.DMA((2,))],
        compiler_params=_cparams(("arbitrary",)),
        name="moe_combine",
    )(d0, d1, d0, d1, x1, route, y)


def _dispatch_plan(route, t):
    expert = route[:, 0:2].astype(jnp.int32)
    n_assign = 2 * t
    flat_e = expert.reshape(n_assign)
    order = jnp.argsort(flat_e).astype(jnp.int32)
    below = jnp.sum(flat_e[None, :] < jnp.arange(N_EXPERTS + 1, dtype=jnp.int32)[:, None], axis=1)
    start = below[:-1].astype(jnp.int32)
    counts = (below[1:] - below[:-1]).astype(jnp.int32)
    padded = (counts + MOE_BM - 1) // MOE_BM * MOE_BM
    pad_end = jnp.cumsum(padded)
    pad_start = pad_end - padded
    shift = pad_start - start
    shift_step = shift - jnp.concatenate([jnp.zeros((1,), jnp.int32), shift[:-1]])
    pos = jnp.arange(n_assign, dtype=jnp.int32)
    dest_sorted = pos + jnp.sum(jnp.where(pos[:, None] >= start[None, :], shift_step[None, :], 0), axis=1)
    _, dest = lax.sort_key_val(order, dest_sorted)
    dest = dest.reshape(t, 2)
    n_blocks = n_assign // MOE_BM + N_EXPERTS
    nblk = (pad_end[-1] // MOE_BM).astype(jnp.int32).reshape(1)
    blk_start = jnp.arange(n_blocks, dtype=jnp.int32) * MOE_BM
    blk_e = jnp.minimum(
        jnp.sum(pad_end[None, :] <= blk_start[:, None], axis=1), N_EXPERTS - 1).astype(jnp.int32)
    off = (blk_start - pad_start[blk_e])[:, None] + jnp.arange(MOE_BM, dtype=jnp.int32)[None, :]
    src = jnp.clip(start[blk_e][:, None] + off, 0, n_assign - 1).reshape(-1)
    valid = (off < counts[blk_e][:, None]).reshape(-1)
    slot_tok = jnp.where(valid, order[src] // 2, 0).astype(jnp.int32)
    run_end = pad_end[blk_e] // MOE_BM
    nxt_e = jnp.where(run_end < nblk[0], blk_e[jnp.minimum(run_end, n_blocks - 1)], -1).astype(jnp.int32)
    last_e = blk_e[jnp.maximum(nblk[0] - 1, 0)]
    blk_e = jnp.where(jnp.arange(n_blocks) < nblk[0], blk_e, last_e)
    return nblk, blk_e, nxt_e, slot_tok, dest[:, 0], dest[:, 1]


def _layer(x, norm1_w, w_in, q_norm_w, k_norm_w, rel_bias, conv_w, a_log, dt_bias,
           o_norm_w, w_out, norm2_w, w_group, w_router, w_gate, w_up, w_down):
    bsz, seq, d = x.shape
    t = bsz * seq
    assert seq % SEQ_TILE == 0 and d % HEAD_DIM == 0 and t % COMBINE_TM == 0
    assert w_in.shape[1] == MAIN_COLS + 2 * N_HEADS
    x2d = x.reshape(t, d)

    w_in_t = jnp.swapaxes(w_in, 0, 1)
    h1 = _rmsnorm(x2d, norm1_w.astype(F32), BF16)
    p3 = _inproj(h1, w_in_t)
    gcol, grow = _gates(h1, w_in_t[MAIN_COLS:], a_log, dt_bias)

    ya = _band_attention(p3, _attn_bias_tile(rel_bias), q_norm_w, k_norm_w, bsz, seq)
    yb = _gated_delta(p3, conv_w, gcol, grow, o_norm_w, bsz, seq)
    x1 = _outproj(ya, yb, w_out, x2d)

    h2, route = _norm_router(x1, norm2_w, w_group, w_router)
    nblk, blk_e, nxt_e, slot_tok, dest0, dest1 = _dispatch_plan(route, t)
    y = _expert_ffn(h2, w_gate, w_up, w_down, nblk, blk_e, nxt_e, slot_tok)
    out = _combine(x1, route, y, dest0, dest1)
    return out.reshape(bsz, seq, d)


def kernel(x, norm1_w, w_in, q_norm_w, k_norm_w, rel_bias, conv_w, a_log, dt_bias, o_norm_w, w_out,
           norm2_w, w_group, w_router, w_gate, w_up, w_down):
    for l in range(norm1_w.shape[0]):
        x = _layer(x, norm1_w[l], w_in[l], q_norm_w[l], k_norm_w[l], rel_bias[l], conv_w[l], a_log[l],
                   dt_bias[l], o_norm_w[l], w_out[l], norm2_w[l], w_group[l], w_router[l], w_gate[l],
                   w_up[l], w_down[l])
    return x
```

```python
import functools

import jax
import jax.numpy as jnp
import numpy as np
from jax import lax
from jax.experimental import pallas as pl
from jax.experimental.pallas import tpu as pltpu

F32 = jnp.float32
BF16 = jnp.bfloat16

CHUNK = 64
HEAD_DIM = 128
N_HEADS = 16
WIDTH = N_HEADS * HEAD_DIM
LEFT_CHUNKS = 8
REL_CLIP = 256
CONV_K = 4
N_GROUPS = 8
EXPERTS_PER_GROUP = 8
N_EXPERTS = N_GROUPS * EXPERTS_PER_GROUP
D_EXPERT = 512
EPS = 1e-6
NEG = -1e30

BLK_ATT_Q, BLK_ATT_K, BLK_ATT_V = 0, 16, 32
BLK_DN_Q, BLK_DN_K, BLK_DN_V, BLK_DN_Z = 48, 64, 80, 96
N_MAIN_BLKS = 112
MAIN_COLS = N_MAIN_BLKS * HEAD_DIM

SEQ_TILE = 512
COMBINE_TM = 256

V7X_VMEM_LIMIT = 56 * 1024 * 1024


def _cparams(sem, vmem=V7X_VMEM_LIMIT):
    return pltpu.CompilerParams(dimension_semantics=sem, vmem_limit_bytes=vmem)


def _dot_nt(a, b):
    return lax.dot_general(a, b, (((1,), (1,)), ((), ())), preferred_element_type=F32)


def _rmsnorm_body(x_ref, w_ref, o_ref):
    x = x_ref[...]
    ms = jnp.mean(x * x, axis=-1, keepdims=True)
    o_ref[...] = (x * lax.rsqrt(ms + EPS) * w_ref[...]).astype(o_ref.dtype)


def _rmsnorm(x2d, w, out_dtype):
    t, d = x2d.shape
    tm = min(256, t)
    return pl.pallas_call(
        _rmsnorm_body,
        out_shape=jax.ShapeDtypeStruct((t, d), out_dtype),
        grid=(t // tm,),
        in_specs=[pl.BlockSpec((tm, d), lambda i: (i, 0)), pl.BlockSpec((1, d), lambda i: (0, 0))],
        out_specs=pl.BlockSpec((tm, d), lambda i: (i, 0)),
        compiler_params=_cparams(("parallel",)),
        name="rmsnorm",
    )(x2d, w.reshape(1, d))


def _inproj_body(h_ref, wt_ref, o_ref):
    acc = _dot_nt(h_ref[...], wt_ref[...].astype(BF16))
    for c in range(o_ref.shape[0]):
        o_ref[c] = acc[:, c * HEAD_DIM:(c + 1) * HEAD_DIM].astype(o_ref.dtype)


def _inproj(h, w_in_t):
    t, d = h.shape
    tm = min(1024, t)
    tn = 512
    nb = tn // HEAD_DIM
    return pl.pallas_call(
        _inproj_body,
        out_shape=jax.ShapeDtypeStruct((N_MAIN_BLKS, t, HEAD_DIM), BF16),
        grid=(t // tm, MAIN_COLS // tn),
        in_specs=[pl.BlockSpec((tm, d), lambda i, j: (i, 0)), pl.BlockSpec((tn, d), lambda i, j: (j, 0))],
        out_specs=pl.BlockSpec((nb, tm, HEAD_DIM), lambda i, j: (j, i, 0)),
        compiler_params=_cparams(("parallel", "arbitrary")),
        name="inproj",
    )(h, w_in_t)


def _chunk_cumsum_rows(x):
    row = lax.broadcasted_iota(jnp.int32, x.shape, 0) % CHUNK
    shift = 1
    while shift < CHUNK:
        x = x + jnp.where(row >= shift, pltpu.roll(x, shift, 0), 0.0)
        shift *= 2
    return x


def _gates_body(h_ref, wt_ref, a_ref, dtb_ref, col_ref, row_ref):
    p = _dot_nt(h_ref[...], wt_ref[...].astype(BF16))
    lane = lax.broadcasted_iota(jnp.int32, p.shape, 1)
    beta = jax.nn.sigmoid(p)
    z = p + dtb_ref[...]
    softplus = jnp.maximum(z, 0.0) + jnp.log1p(jnp.exp(-jnp.abs(z)))
    g = -a_ref[...] * softplus
    out = jnp.where(lane < N_HEADS, beta, _chunk_cumsum_rows(g))
    col_ref[...] = out
    row_ref[...] = out.T[N_HEADS:2 * N_HEADS, :]


def _gates(h, w_small_t, a_log, dt_bias):
    t, d = h.shape
    tm = min(1024, t)
    pad = HEAD_DIM - 2 * N_HEADS
    w = jnp.pad(w_small_t, ((0, pad), (0, 0)))
    a = jnp.pad(jnp.exp(a_log.astype(F32)), (N_HEADS, pad)).reshape(1, HEAD_DIM)
    dtb = jnp.pad(dt_bias.astype(F32), (N_HEADS, pad)).reshape(1, HEAD_DIM)
    return pl.pallas_call(
        _gates_body,
        out_shape=(jax.ShapeDtypeStruct((t, HEAD_DIM), F32), jax.ShapeDtypeStruct((N_HEADS, t), F32)),
        grid=(t // tm,),
        in_specs=[
            pl.BlockSpec((tm, d), lambda i: (i, 0)),
            pl.BlockSpec((HEAD_DIM, d), lambda i: (0, 0)),
            pl.BlockSpec((1, HEAD_DIM), lambda i: (0, 0)),
            pl.BlockSpec((1, HEAD_DIM), lambda i: (0, 0)),
        ],
        out_specs=(pl.BlockSpec((tm, HEAD_DIM), lambda i: (i, 0)), pl.BlockSpec((N_HEADS, tm), lambda i: (0, i))),
        compiler_params=_cparams(("parallel",)),
        name="gates",
    )(h, w, a, dtb)


ATT_SUB = 128
ATT_WIN = ATT_SUB + LEFT_CHUNKS * CHUNK


def _attn_bias_tile(rel_bias):
    period = ATT_SUB + ATT_WIN
    offs = np.arange(period)
    key_minus_query = np.where(offs < ATT_WIN, offs, offs - period)
    rel = LEFT_CHUNKS * CHUNK - key_minus_query
    diag_vals = rel_bias[:, np.clip(rel, -REL_CLIP, REL_CLIP) + REL_CLIP].astype(F32)
    flat = jnp.tile(diag_vals, (1, ATT_SUB))[:, :ATT_SUB * (period - 1)]
    bias = flat.reshape(-1, ATT_SUB, period - 1)[:, :, :ATT_WIN]
    qc = np.arange(ATT_SUB)[:, None] // CHUNK
    kc = np.arange(ATT_WIN)[None, :] // CHUNK
    allowed = (kc >= qc) & (kc <= qc + LEFT_CHUNKS)
    return jnp.where(allowed[None], bias, NEG)


def _head_rmsnorm(x, w):
    xf = x.astype(F32)
    return xf * lax.rsqrt(jnp.mean(xf * xf, axis=-1, keepdims=True) + EPS) * w


def _attn_body(q_ref, kp_ref, kc_ref, vp_ref, vc_ref, bias_ref, qw_ref, kw_ref, o_ref):
    t = pl.program_id(1)
    q = _head_rmsnorm(q_ref[...], qw_ref[...]).astype(BF16)
    k = jnp.concatenate(
        [_head_rmsnorm(kp_ref[...], kw_ref[...]), _head_rmsnorm(kc_ref[...], kw_ref[...])], axis=0
    ).astype(BF16)
    v = jnp.concatenate([vp_ref[...], vc_ref[...]], axis=0)
    n_sub = SEQ_TILE // ATT_SUB
    qb = q.reshape(n_sub, ATT_SUB, HEAD_DIM)
    kb = jnp.stack([k[p * ATT_SUB:p * ATT_SUB + ATT_WIN] for p in range(n_sub)], axis=0)
    vb = jnp.stack([v[p * ATT_SUB:p * ATT_SUB + ATT_WIN] for p in range(n_sub)], axis=0)
    s = jnp.einsum("pqd,pkd->pqk", qb, kb, preferred_element_type=F32) + bias_ref[...][None]
    col = lax.broadcasted_iota(jnp.int32, (n_sub, ATT_SUB, ATT_WIN), 2)
    sub = lax.broadcasted_iota(jnp.int32, (n_sub, ATT_SUB, ATT_WIN), 0)
    first_valid = jnp.where(t > 0, 0, SEQ_TILE) - sub * jnp.where(t > 0, 0, ATT_SUB)
    s = jnp.where(col >= first_valid, s, NEG)
    m = jnp.max(s, axis=-1, keepdims=True)
    e = jnp.exp(s - m)
    l = jnp.sum(e, axis=-1, keepdims=True)
    o = jnp.einsum("pqk,pkd->pqd", e.astype(BF16), vb, preferred_element_type=F32) / l
    o_ref[...] = o.reshape(SEQ_TILE, HEAD_DIM).astype(o_ref.dtype)


def _band_attention(p3, bias_tile, q_norm_w, k_norm_w, bsz, seq):
    nt = seq // SEQ_TILE
    qw = (q_norm_w.astype(F32) * (HEAD_DIM ** -0.5)).reshape(1, HEAD_DIM)
    kw = k_norm_w.astype(F32).reshape(1, HEAD_DIM)

    def cur(blk):
        return lambda bh, t: (blk + bh % N_HEADS, (bh // N_HEADS) * nt + t, 0)

    def prev(blk):
        return lambda bh, t: (blk + bh % N_HEADS, (bh // N_HEADS) * nt + jnp.maximum(t - 1, 0), 0)

    tile = (None, SEQ_TILE, HEAD_DIM)
    return pl.pallas_call(
        _attn_body,
        out_shape=jax.ShapeDtypeStruct((bsz * seq, WIDTH), BF16),
        grid=(bsz * N_HEADS, nt),
        in_specs=[
            pl.BlockSpec(tile, cur(BLK_ATT_Q)),
            pl.BlockSpec(tile, prev(BLK_ATT_K)),
            pl.BlockSpec(tile, cur(BLK_ATT_K)),
            pl.BlockSpec(tile, prev(BLK_ATT_V)),
            pl.BlockSpec(tile, cur(BLK_ATT_V)),
            pl.BlockSpec((None, ATT_SUB, ATT_WIN), lambda bh, t: (bh % N_HEADS, 0, 0)),
            pl.BlockSpec((1, HEAD_DIM), lambda bh, t: (0, 0)),
            pl.BlockSpec((1, HEAD_DIM), lambda bh, t: (0, 0)),
        ],
        out_specs=pl.BlockSpec((SEQ_TILE, HEAD_DIM), lambda bh, t: ((bh // N_HEADS) * nt + t, bh % N_HEADS)),
        compiler_params=_cparams(("parallel", "arbitrary")),
        name="band_attention",
    )(p3, p3, p3, p3, p3, bias_tile, qw, kw)


CONV_HALO = 16


def _short_conv_silu(cur, halo, w, t, cat_ref):
    cur = cur.astype(F32)
    cat_ref[0:CONV_HALO, :] = jnp.where(t > 0, halo.astype(F32), 0.0)
    cat_ref[CONV_HALO:, :] = cur
    y = cur * w[CONV_K - 1:CONV_K]
    for i in range(CONV_K - 1):
        shift = CONV_K - 1 - i
        y = y + cat_ref[CONV_HALO - shift:CONV_HALO - shift + SEQ_TILE, :] * w[i:i + 1]
    return y * jax.nn.sigmoid(y)


def _l2norm(x):
    return x * lax.rsqrt(jnp.sum(x * x, axis=-1, keepdims=True) + EPS)


def _bdot(a, b):
    return jnp.dot(a.astype(BF16), b.astype(BF16), preferred_element_type=F32)


def _bdot_nt(a, b):
    return lax.dot_general(a.astype(BF16), b.astype(BF16), (((1,), (1,)), ((), ())), preferred_element_type=F32)


GDN_HB = 8


def _bmm(a, b):
    return jnp.einsum("gmk,gkn->gmn", a.astype(BF16), b.astype(BF16), preferred_element_type=F32)


def _bmm_nt(a, b):
    return jnp.einsum("gmk,gnk->gmn", a.astype(BF16), b.astype(BF16), preferred_element_type=F32)


def _gdn_chunk_terms(q, k, v, beta, gc, grow):
    ri = lax.broadcasted_iota(jnp.int32, (1, CHUNK, CHUNK), 1)
    ci = lax.broadcasted_iota(jnp.int32, (1, CHUNK, CHUNK), 2)
    gamma = jnp.exp(jnp.where(ri >= ci, gc - grow, NEG))
    kb = k * beta
    kqk = _bmm_nt(jnp.concatenate([kb, q], axis=1), k)
    m = jnp.where(ri > ci, kqk[:, :CHUNK] * gamma, 0.0)
    att = kqk[:, CHUNK:] * gamma
    inv = (ri == ci).astype(F32) - m
    pw = _bmm(m, m)
    for _ in range(4):
        both = _bmm(jnp.concatenate([inv, pw], axis=1), pw)
        inv = inv + both[:, :CHUNK]
        pw = both[:, CHUNK:]
    inv = inv + _bmm(inv, pw)
    eg = jnp.exp(gc)
    g_last = gc[:, CHUNK - 1:CHUNK]
    sol = _bmm(inv, jnp.concatenate([v * beta, kb * eg], axis=2))
    att_uw = _bmm(att, sol)
    k_dec = k * jnp.exp(g_last - gc)
    kd_uw = _bmm(jnp.swapaxes(k_dec, 1, 2), sol)
    lhs = jnp.concatenate([q * eg - att_uw[:, :, HEAD_DIM:], kd_uw[:, :, HEAD_DIM:]], axis=1)
    return lhs, att_uw[:, :, :HEAD_DIM], kd_uw[:, :, :HEAD_DIM], jnp.exp(g_last)


def _gdn_body(q_ref, k_ref, v_ref, qh_ref, kh_ref, vh_ref, wq_ref, wk_ref, wv_ref, z_ref,
              gcol_ref, grow_ref, ow_ref, o_ref, state_ref, cat_ref):
    group = pl.program_id(0) % (N_HEADS // GDN_HB)
    t = pl.program_id(1)

    @pl.when(t == 0)
    def _():
        state_ref[...] = jnp.zeros_like(state_ref)

    gates = gcol_ref[...]
    lane = lax.broadcasted_iota(jnp.int32, gates.shape, 1)
    n_chunks = SEQ_TILE // CHUNK
    n_prob = GDN_HB * n_chunks

    qs, ks, vs, betas, gcs, grows = [], [], [], [], [], []
    for hh in range(GDN_HB):
        head = group * GDN_HB + hh
        qs.append(_l2norm(_short_conv_silu(q_ref[hh], qh_ref[hh], wq_ref[hh], t, cat_ref.at[3 * hh]))
                  * (HEAD_DIM ** -0.5))
        ks.append(_l2norm(_short_conv_silu(k_ref[hh], kh_ref[hh], wk_ref[hh], t, cat_ref.at[3 * hh + 1])))
        vs.append(_short_conv_silu(v_ref[hh], vh_ref[hh], wv_ref[hh], t, cat_ref.at[3 * hh + 2]))
        betas.append(jnp.sum(jnp.where(lane == head, gates, 0.0), axis=-1, keepdims=True))
        gcs.append(jnp.sum(jnp.where(lane == head + N_HEADS, gates, 0.0), axis=-1, keepdims=True))
        grow = grow_ref[hh]
        grows.extend(grow[:, c * CHUNK:(c + 1) * CHUNK] for c in range(n_chunks))

    def chunked(parts):
        return jnp.stack(parts, axis=0).reshape(n_prob, CHUNK, parts[0].shape[-1])

    lhs, o_intra, s_add, decay = _gdn_chunk_terms(
        chunked(qs), chunked(ks), chunked(vs), chunked(betas), chunked(gcs), jnp.stack(grows, axis=0))
    lhs = lhs.reshape(GDN_HB, n_chunks, CHUNK + HEAD_DIM, HEAD_DIM)
    o_intra = o_intra.reshape(GDN_HB, n_chunks, CHUNK, HEAD_DIM)
    s_add = s_add.reshape(GDN_HB, n_chunks, HEAD_DIM, HEAD_DIM)
    decay = decay.reshape(GDN_HB, n_chunks, 1, 1)

    state = state_ref[...]
    ow = ow_ref[...]
    for c in range(n_chunks):
        prod = _bmm(lhs[:, c], state)
        o_c = prod[:, :CHUNK] + o_intra[:, c]
        state = state * decay[:, c] - prod[:, CHUNK:] + s_add[:, c]
        o_n = o_c * lax.rsqrt(jnp.mean(o_c * o_c, axis=-1, keepdims=True) + EPS) * ow
        for hh in range(GDN_HB):
            zc = z_ref[hh, c * CHUNK:(c + 1) * CHUNK, :].astype(F32)
            o_ref[c * CHUNK:(c + 1) * CHUNK, hh * HEAD_DIM:(hh + 1) * HEAD_DIM] = (
                o_n[hh] * (zc * jax.nn.sigmoid(zc))).astype(o_ref.dtype)
    state_ref[...] = state


def _gated_delta(p3, conv_w, gcol, grow, o_norm_w, bsz, seq):
    nt = seq // SEQ_TILE
    halos_per_tile = SEQ_TILE // CONV_HALO
    n_groups = N_HEADS // GDN_HB
    cw = conv_w.astype(F32).reshape(CONV_K, 3 * N_HEADS, HEAD_DIM).transpose(1, 0, 2)
    grow3 = grow.reshape(N_HEADS, 1, bsz * seq)

    def cur(blk):
        return lambda bg, t: (blk // GDN_HB + bg % n_groups, (bg // n_groups) * nt + t, 0)

    def halo(blk):
        return lambda bg, t: (
            blk // GDN_HB + bg % n_groups, jnp.maximum(((bg // n_groups) * nt + t) * halos_per_tile - 1, 0), 0)

    def cwmap(off):
        return lambda bg, t: (off // GDN_HB + bg % n_groups, 0, 0)

    tile = (GDN_HB, SEQ_TILE, HEAD_DIM)
    htile = (GDN_HB, CONV_HALO, HEAD_DIM)
    wtile = (GDN_HB, CONV_K, HEAD_DIM)
    return pl.pallas_call(
        _gdn_body,
        out_shape=jax.ShapeDtypeStruct((bsz * seq, WIDTH), BF16),
        grid=(bsz * n_groups, nt),
        in_specs=[
            pl.BlockSpec(tile, cur(BLK_DN_Q)),
            pl.BlockSpec(tile, cur(BLK_DN_K)),
            pl.BlockSpec(tile, cur(BLK_DN_V)),
            pl.BlockSpec(htile, halo(BLK_DN_Q)),
            pl.BlockSpec(htile, halo(BLK_DN_K)),
            pl.BlockSpec(htile, halo(BLK_DN_V)),
            pl.BlockSpec(wtile, cwmap(0)),
            pl.BlockSpec(wtile, cwmap(N_HEADS)),
            pl.BlockSpec(wtile, cwmap(2 * N_HEADS)),
            pl.BlockSpec(tile, cur(BLK_DN_Z)),
            pl.BlockSpec((SEQ_TILE, HEAD_DIM), lambda bg, t: ((bg // n_groups) * nt + t, 0)),
            pl.BlockSpec((GDN_HB, 1, SEQ_TILE), lambda bg, t: (bg % n_groups, 0, (bg // n_groups) * nt + t)),
            pl.BlockSpec((1, HEAD_DIM), lambda bg, t: (0, 0)),
        ],
        out_specs=pl.BlockSpec(
            (SEQ_TILE, GDN_HB * HEAD_DIM), lambda bg, t: ((bg // n_groups) * nt + t, bg % n_groups)),
        scratch_shapes=[
            pltpu.VMEM((GDN_HB, HEAD_DIM, HEAD_DIM), F32),
            pltpu.VMEM((3 * GDN_HB, CONV_HALO + SEQ_TILE, HEAD_DIM), F32),
        ],
        compiler_params=_cparams(("parallel", "arbitrary")),
        name="gated_delta",
    )(p3, p3, p3, p3, p3, p3, cw, cw, cw, p3, gcol, grow3, o_norm_w.astype(F32).reshape(1, HEAD_DIM))


def _outproj_body(ya_ref, yb_ref, wa_ref, wb_ref, x_ref, o_ref):
    acc = jnp.dot(ya_ref[...], wa_ref[...].astype(BF16), preferred_element_type=F32)
    acc = acc + jnp.dot(yb_ref[...], wb_ref[...].astype(BF16), preferred_element_type=F32)
    o_ref[...] = x_ref[...] + acc


def _outproj(ya, yb, w_out, x2d):
    t, d = x2d.shape
    tm = min(1024, t)
    tn = min(512, d)
    return pl.pallas_call(
        _outproj_body,
        out_shape=jax.ShapeDtypeStruct((t, d), F32),
        grid=(t // tm, d // tn),
        in_specs=[
            pl.BlockSpec((tm, WIDTH), lambda i, j: (i, 0)),
            pl.BlockSpec((tm, WIDTH), lambda i, j: (i, 0)),
            pl.BlockSpec((WIDTH, tn), lambda i, j: (0, j)),
            pl.BlockSpec((WIDTH, tn), lambda i, j: (1, j)),
            pl.BlockSpec((tm, tn), lambda i, j: (i, j)),
        ],
        out_specs=pl.BlockSpec((tm, tn), lambda i, j: (i, j)),
        compiler_params=_cparams(("parallel", "arbitrary")),
        name="outproj",
    )(ya, yb, w_out, w_out, x2d)


def _router_body(x_ref, nw_ref, wr_ref, h_ref, r_ref):
    x = x_ref[...]
    h = x * lax.rsqrt(jnp.mean(x * x, axis=-1, keepdims=True) + EPS) * nw_ref[...]
    h_ref[...] = h
    w = wr_ref[...]
    h_hi, w_hi = h.astype(BF16), w.astype(BF16)
    h_lo = (h - h_hi.astype(F32)).astype(BF16)
    w_lo = (w - w_hi.astype(F32)).astype(BF16)
    logits = (jnp.dot(h_hi, w_hi, preferred_element_type=F32) + jnp.dot(h_hi, w_lo, preferred_element_type=F32)
              + jnp.dot(h_lo, w_hi, preferred_element_type=F32))
    lane = lax.broadcasted_iota(jnp.int32, logits.shape, 1).astype(F32)
    big = float(HEAD_DIM)

    def first_argmax(vals, vmax):
        return jnp.min(jnp.where(vals == vmax, lane, big), axis=-1, keepdims=True)

    gl = jnp.where(lane < N_GROUPS, logits, NEG)
    gmax = jnp.max(gl, axis=-1, keepdims=True)
    gidx = first_argmax(gl, gmax)
    grp_w = 1.0 / jnp.sum(jnp.exp(gl - gmax), axis=-1, keepdims=True)
    lo = N_GROUPS + gidx * EXPERTS_PER_GROUP
    el = jnp.where((lane >= lo) & (lane < lo + EXPERTS_PER_GROUP), logits, NEG)
    m1 = jnp.max(el, axis=-1, keepdims=True)
    i1 = first_argmax(el, m1)
    el2 = jnp.where(lane == i1, NEG, el)
    m2 = jnp.max(el2, axis=-1, keepdims=True)
    i2 = first_argmax(el2, m2)
    r = jnp.exp(m2 - m1)
    w1 = 1.0 / (1.0 + r)
    w2 = r / (1.0 + r)
    out = jnp.where(lane == 0, i1 - N_GROUPS, 0.0)
    out = jnp.where(lane == 1, i2 - N_GROUPS, out)
    out = jnp.where(lane == 2, grp_w * w1, out)
    out = jnp.where(lane == 3, grp_w * w2, out)
    r_ref[...] = out


def _norm_router(x1, norm_w, w_group, w_router):
    t, d = x1.shape
    tm = min(256, t)
    wr = jnp.concatenate([w_group, w_router], axis=1).astype(F32)
    wr = jnp.pad(wr, ((0, 0), (0, HEAD_DIM - wr.shape[1])))
    return pl.pallas_call(
        _router_body,
        out_shape=(jax.ShapeDtypeStruct((t, d), F32), jax.ShapeDtypeStruct((t, HEAD_DIM), F32)),
        grid=(t // tm,),
        in_specs=[
            pl.BlockSpec((tm, d), lambda i: (i, 0)),
            pl.BlockSpec((1, d), lambda i: (0, 0)),
            pl.BlockSpec((d, HEAD_DIM), lambda i: (0, 0)),
        ],
        out_specs=(pl.BlockSpec((tm, d), lambda i: (i, 0)), pl.BlockSpec((tm, HEAD_DIM), lambda i: (i, 0))),
        compiler_params=_cparams(("parallel",)),
        name="norm_router",
    )(x1, norm_w.astype(F32).reshape(1, d), wr)


def _row_gather_start(src_hbm, idx_ref, dst, sem, row0, n_rows):
    def body(r, carry):
        rr = row0 + r
        pltpu.make_async_copy(src_hbm.at[pl.ds(idx_ref[0, rr], 1)], dst.at[pl.ds(rr, 1)], sem).start()
        return carry

    lax.fori_loop(0, n_rows, body, 0, unroll=8)


def _row_gather_wait(dst, sem):
    pltpu.make_async_copy(dst, dst, sem).wait()


MOE_BM = 256
FFN_CAST_CHUNKS = 8


def _cast_rows(src, dst, dst_row0=0):
    rows = src.shape[0] // FFN_CAST_CHUNKS
    for c in range(FFN_CAST_CHUNKS):
        dst[dst_row0 + c * rows:dst_row0 + (c + 1) * rows, :] = src[c * rows:(c + 1) * rows, :].astype(BF16)


def _ffn_body(nblk_ref, blke_ref, nxte_ref, tok_ref, tokn_ref, x_hbm, wg_hbm, wu_hbm, wd_hbm, y_ref,
              xbuf, wg_f32, wu_f32, wd_half, wg_bf, wu_bf, wd_bf, sem, wsem):
    i = pl.program_id(0)
    nblk = nblk_ref[0]
    slot = i % 2
    half = wd_half.shape[0]

    def gate_up_copies(e):
        return (pltpu.make_async_copy(wg_hbm.at[e], wg_f32, wsem.at[0]),
                pltpu.make_async_copy(wu_hbm.at[e], wu_f32, wsem.at[1]))

    def down_half_copy(e, h):
        return pltpu.make_async_copy(wd_hbm.at[e, pl.ds(h * half, half)], wd_half, wsem.at[2])

    def request_weights(e):
        for c in gate_up_copies(e):
            c.start()
        down_half_copy(e, 0).start()

    @pl.when((i == 0) & (nblk > 0))
    def _():
        _row_gather_start(x_hbm, tok_ref, xbuf.at[0], sem.at[0], 0, MOE_BM)
        request_weights(blke_ref[0])

    @pl.when(i + 1 < nblk)
    def _():
        _row_gather_start(x_hbm, tokn_ref, xbuf.at[1 - slot], sem.at[1 - slot], 0, MOE_BM)

    new_expert = (i == 0) | (blke_ref[i] != blke_ref[jnp.maximum(i - 1, 0)])

    @pl.when((i < nblk) & new_expert)
    def _():
        e = blke_ref[i]
        down_half_copy(e, 0).wait()
        _cast_rows(wd_half, wd_bf, 0)
        down_half_copy(e, 1).start()
        for c in gate_up_copies(e):
            c.wait()
        _cast_rows(wg_f32, wg_bf)
        _cast_rows(wu_f32, wu_bf)
        down_half_copy(e, 1).wait()
        _cast_rows(wd_half, wd_bf, half)

        @pl.when(nxte_ref[i] >= 0)
        def _():
            request_weights(nxte_ref[i])

    @pl.when(i < nblk)
    def _():
        _row_gather_wait(xbuf.at[slot], sem.at[slot])
        x = xbuf[slot].astype(BF16)
        g = jnp.dot(x, wg_bf[...], preferred_element_type=F32)
        u = jnp.dot(x, wu_bf[...], preferred_element_type=F32)
        hid = (g * jax.nn.sigmoid(g) * u).astype(BF16)
        y_ref[...] = jnp.dot(hid, wd_bf[...], preferred_element_type=F32)

    @pl.when(i >= nblk)
    def _():
        y_ref[...] = jnp.zeros_like(y_ref)


def _expert_ffn(h2, w_gate, w_up, w_down, nblk, blk_e, nxt_e, slot_tok):
    t, d = h2.shape
    n_blocks = blk_e.shape[0]
    tok_arr = slot_tok.reshape(n_blocks, 1, MOE_BM)
    smem_blk = functools.partial(pl.BlockSpec, (None, 1, MOE_BM), memory_space=pltpu.SMEM)
    hbm = pl.BlockSpec(memory_space=pl.ANY)
    grid_spec = pltpu.PrefetchScalarGridSpec(
        num_scalar_prefetch=3,
        grid=(n_blocks,),
        in_specs=[
            smem_blk(lambda i, nb, be, ne: (i, 0, 0)),
            smem_blk(lambda i, nb, be, ne: (jnp.minimum(i + 1, n_blocks - 1), 0, 0)),
            hbm, hbm, hbm, hbm,
        ],
        out_specs=pl.BlockSpec((MOE_BM, d), lambda i, nb, be, ne: (i, 0)),
        scratch_shapes=[
            pltpu.VMEM((2, MOE_BM, d), F32),
            pltpu.VMEM((d, D_EXPERT), F32),
            pltpu.VMEM((d, D_EXPERT), F32),
            pltpu.VMEM((D_EXPERT // 2, d), F32),
            pltpu.VMEM((d, D_EXPERT), BF16),
            pltpu.VMEM((d, D_EXPERT), BF16),
            pltpu.VMEM((D_EXPERT, d), BF16),
            pltpu.SemaphoreType.DMA((2,)),
            pltpu.SemaphoreType.DMA((3,)),
        ],
    )
    return pl.pallas_call(
        _ffn_body,
        out_shape=jax.ShapeDtypeStruct((n_blocks * MOE_BM, d), F32),
        grid_spec=grid_spec,
        compiler_params=_cparams(("arbitrary",)),
        name="expert_ffn",
    )(nblk, blk_e, nxt_e, tok_arr, tok_arr, h2, w_gate, w_up, w_down)


def _combine_body(d0_ref, d1_ref, d0n_ref, d1n_ref, x_ref, r_ref, y_hbm, o_ref, ybuf, sem):
    i = pl.program_id(0)
    n = pl.num_programs(0)
    slot = i % 2

    @pl.when(i == 0)
    def _():
        _row_gather_start(y_hbm, d0_ref, ybuf.at[0, 0], sem.at[0], 0, COMBINE_TM)
        _row_gather_start(y_hbm, d1_ref, ybuf.at[0, 1], sem.at[0], 0, COMBINE_TM)

    @pl.when(i + 1 < n)
    def _():
        _row_gather_start(y_hbm, d0n_ref, ybuf.at[1 - slot, 0], sem.at[1 - slot], 0, COMBINE_TM)
        _row_gather_start(y_hbm, d1n_ref, ybuf.at[1 - slot, 1], sem.at[1 - slot], 0, COMBINE_TM)

    _row_gather_wait(ybuf.at[slot], sem.at[slot])
    route = r_ref[...]
    o_ref[...] = x_ref[...] + route[:, 2:3] * ybuf[slot, 0] + route[:, 3:4] * ybuf[slot, 1]


def _combine(x1, route, y, dest0, dest1):
    t, d = x1.shape
    tm = min(COMBINE_TM, t)
    assert tm == COMBINE_TM
    n = t // tm
    d0 = dest0.reshape(n, 1, tm)
    d1 = dest1.reshape(n, 1, tm)
    smem_blk = functools.partial(pl.BlockSpec, (None, 1, tm), memory_space=pltpu.SMEM)
    nxt = lambda i: (jnp.minimum(i + 1, n - 1), 0, 0)
    return pl.pallas_call(
        _combine_body,
        out_shape=jax.ShapeDtypeStruct((t, d), F32),
        grid=(n,),
        in_specs=[
            smem_blk(lambda i: (i, 0, 0)),
            smem_blk(lambda i: (i, 0, 0)),
            smem_blk(nxt),
            smem_blk(nxt),
            pl.BlockSpec((tm, d), lambda i: (i, 0)),
            pl.BlockSpec((tm, HEAD_DIM), lambda i: (i, 0)),
            pl.BlockSpec(memory_space=pl.ANY),
        ],
        out_specs=pl.BlockSpec((tm, d), lambda i: (i, 0)),
        scratch_shapes=[pltpu.VMEM((2, 2, tm, d), F32), pltpu.SemaphoreType.DMA((2,))],
        compiler_params=_cparams(("arbitrary",)),
        name="moe_combine",
    )(d0, d1, d0, d1, x1, route, y)


def _dispatch_plan(route, t):
    expert = route[:, 0:2].astype(jnp.int32)
    n_assign = 2 * t
    flat_e = expert.reshape(n_assign)
    order = jnp.argsort(flat_e).astype(jnp.int32)
    below = jnp.sum(flat_e[None, :] < jnp.arange(N_EXPERTS + 1, dtype=jnp.int32)[:, None], axis=1)
    start = below[:-1].astype(jnp.int32)
    counts = (below[1:] - below[:-1]).astype(jnp.int32)
    padded = (counts + MOE_BM - 1) // MOE_BM * MOE_BM
    pad_end = jnp.cumsum(padded)
    pad_start = pad_end - padded
    shift = pad_start - start
    shift_step = shift - jnp.concatenate([jnp.zeros((1,), jnp.int32), shift[:-1]])
    pos = jnp.arange(n_assign, dtype=jnp.int32)
    dest_sorted = pos + jnp.sum(jnp.where(pos[:, None] >= start[None, :], shift_step[None, :], 0), axis=1)
    _, dest = lax.sort_key_val(order, dest_sorted)
    dest = dest.reshape(t, 2)
    n_blocks = n_assign // MOE_BM + N_EXPERTS
    nblk = (pad_end[-1] // MOE_BM).astype(jnp.int32).reshape(1)
    blk_start = jnp.arange(n_blocks, dtype=jnp.int32) * MOE_BM
    blk_e = jnp.minimum(
        jnp.sum(pad_end[None, :] <= blk_start[:, None], axis=1), N_EXPERTS - 1).astype(jnp.int32)
    off = (blk_start - pad_start[blk_e])[:, None] + jnp.arange(MOE_BM, dtype=jnp.int32)[None, :]
    src = jnp.clip(start[blk_e][:, None] + off, 0, n_assign - 1).reshape(-1)
    valid = (off < counts[blk_e][:, None]).reshape(-1)
    slot_tok = jnp.where(valid, order[src] // 2, 0).astype(jnp.int32)
    run_end = pad_end[blk_e] // MOE_BM
    nxt_e = jnp.where(run_end < nblk[0], blk_e[jnp.minimum(run_end, n_blocks - 1)], -1).astype(jnp.int32)
    last_e = blk_e[jnp.maximum(nblk[0] - 1, 0)]
    blk_e = jnp.where(jnp.arange(n_blocks) < nblk[0], blk_e, last_e)
    return nblk, blk_e, nxt_e, slot_tok, dest[:, 0], dest[:, 1]


def _layer(x, norm1_w, w_in, q_norm_w, k_norm_w, rel_bias, conv_w, a_log, dt_bias,
           o_norm_w, w_out, norm2_w, w_group, w_router, w_gate, w_up, w_down):
    bsz, seq, d = x.shape
    t = bsz * seq
    assert seq % SEQ_TILE == 0 and d % HEAD_DIM == 0 and t % COMBINE_TM == 0
    assert w_in.shape[1] == MAIN_COLS + 2 * N_HEADS
    x2d = x.reshape(t, d)

    w_in_t = jnp.swapaxes(w_in, 0, 1)
    h1 = _rmsnorm(x2d, norm1_w.astype(F32), BF16)
    p3 = _inproj(h1, w_in_t)
    gcol, grow = _gates(h1, w_in_t[MAIN_COLS:], a_log, dt_bias)

    ya = _band_attention(p3, _attn_bias_tile(rel_bias), q_norm_w, k_norm_w, bsz, seq)
    yb = _gated_delta(p3, conv_w, gcol, grow, o_norm_w, bsz, seq)
    x1 = _outproj(ya, yb, w_out, x2d)

    h2, route = _norm_router(x1, norm2_w, w_group, w_router)
    nblk, blk_e, nxt_e, slot_tok, dest0, dest1 = _dispatch_plan(route, t)
    y = _expert_ffn(h2, w_gate, w_up, w_down, nblk, blk_e, nxt_e, slot_tok)
    out = _combine(x1, route, y, dest0, dest1)
    return out.reshape(bsz, seq, d)


def kernel(x, norm1_w, w_in, q_norm_w, k_norm_w, rel_bias, conv_w, a_log, dt_bias, o_norm_w, w_out,
           norm2_w, w_group, w_router, w_gate, w_up, w_down):
    for l in range(norm1_w.shape[0]):
        x = _layer(x, norm1_w[l], w_in[l], q_norm_w[l], k_norm_w[l], rel_bias[l], conv_w[l], a_log[l],
                   dt_bias[l], o_norm_w[l], w_out[l], norm2_w[l], w_group[l], w_router[l], w_gate[l],
                   w_up[l], w_down[l])
    return x
```

```python
import functools

import jax
import jax.numpy as jnp
import numpy as np
from jax import lax
from jax.experimental import pallas as pl
from jax.experimental.pallas import tpu as pltpu

F32 = jnp.float32
BF16 = jnp.bfloat16

CHUNK = 64
HEAD_DIM = 128
N_HEADS = 16
WIDTH = N_HEADS * HEAD_DIM
LEFT_CHUNKS = 8
REL_CLIP = 256
CONV_K = 4
N_GROUPS = 8
EXPERTS_PER_GROUP = 8
N_EXPERTS = N_GROUPS * EXPERTS_PER_GROUP
D_EXPERT = 512
EPS = 1e-6
NEG = -1e30

BLK_ATT_Q, BLK_ATT_K, BLK_ATT_V = 0, 16, 32
BLK_DN_Q, BLK_DN_K, BLK_DN_V, BLK_DN_Z = 48, 64, 80, 96
N_MAIN_BLKS = 112
MAIN_COLS = N_MAIN_BLKS * HEAD_DIM

SEQ_TILE = 512
COMBINE_TM = 256

V7X_VMEM_LIMIT = 56 * 1024 * 1024


def _cparams(sem, vmem=V7X_VMEM_LIMIT):
    return pltpu.CompilerParams(dimension_semantics=sem, vmem_limit_bytes=vmem)


def _dot_nt(a, b):
    return lax.dot_general(a, b, (((1,), (1,)), ((), ())), preferred_element_type=F32)


def _rmsnorm_body(x_ref, w_ref, o_ref):
    x = x_ref[...]
    ms = jnp.mean(x * x, axis=-1, keepdims=True)
    o_ref[...] = (x * lax.rsqrt(ms + EPS) * w_ref[...]).astype(o_ref.dtype)


def _rmsnorm(x2d, w, out_dtype):
    t, d = x2d.shape
    tm = min(256, t)
    return pl.pallas_call(
        _rmsnorm_body,
        out_shape=jax.ShapeDtypeStruct((t, d), out_dtype),
        grid=(t // tm,),
        in_specs=[pl.BlockSpec((tm, d), lambda i: (i, 0)), pl.BlockSpec((1, d), lambda i: (0, 0))],
        out_specs=pl.BlockSpec((tm, d), lambda i: (i, 0)),
        compiler_params=_cparams(("parallel",)),
        name="rmsnorm",
    )(x2d, w.reshape(1, d))


def _inproj_body(h_ref, wt_ref, o_ref):
    acc = _dot_nt(h_ref[...], wt_ref[...].astype(BF16))
    for c in range(o_ref.shape[0]):
        o_ref[c] = acc[:, c * HEAD_DIM:(c + 1) * HEAD_DIM].astype(o_ref.dtype)


def _inproj(h, w_in_t):
    t, d = h.shape
    tm = min(1024, t)
    tn = 512
    nb = tn // HEAD_DIM
    return pl.pallas_call(
        _inproj_body,
        out_shape=jax.ShapeDtypeStruct((N_MAIN_BLKS, t, HEAD_DIM), BF16),
        grid=(t // tm, MAIN_COLS // tn),
        in_specs=[pl.BlockSpec((tm, d), lambda i, j: (i, 0)), pl.BlockSpec((tn, d), lambda i, j: (j, 0))],
        out_specs=pl.BlockSpec((nb, tm, HEAD_DIM), lambda i, j: (j, i, 0)),
        compiler_params=_cparams(("parallel", "arbitrary")),
        name="inproj",
    )(h, w_in_t)


def _chunk_cumsum_rows(x):
    row = lax.broadcasted_iota(jnp.int32, x.shape, 0) % CHUNK
    shift = 1
    while shift < CHUNK:
        x = x + jnp.where(row >= shift, pltpu.roll(x, shift, 0), 0.0)
        shift *= 2
    return x


def _gates_body(h_ref, wt_ref, a_ref, dtb_ref, col_ref, row_ref):
    p = _dot_nt(h_ref[...], wt_ref[...].astype(BF16))
    lane = lax.broadcasted_iota(jnp.int32, p.shape, 1)
    beta = jax.nn.sigmoid(p)
    z = p + dtb_ref[...]
    softplus = jnp.maximum(z, 0.0) + jnp.log1p(jnp.exp(-jnp.abs(z)))
    g = -a_ref[...] * softplus
    out = jnp.where(lane < N_HEADS, beta, _chunk_cumsum_rows(g))
    col_ref[...] = out
    row_ref[...] = out.T[N_HEADS:2 * N_HEADS, :]


def _gates(h, w_small_t, a_log, dt_bias):
    t, d = h.shape
    tm = min(1024, t)
    pad = HEAD_DIM - 2 * N_HEADS
    w = jnp.pad(w_small_t, ((0, pad), (0, 0)))
    a = jnp.pad(jnp.exp(a_log.astype(F32)), (N_HEADS, pad)).reshape(1, HEAD_DIM)
    dtb = jnp.pad(dt_bias.astype(F32), (N_HEADS, pad)).reshape(1, HEAD_DIM)
    return pl.pallas_call(
        _gates_body,
        out_shape=(jax.ShapeDtypeStruct((t, HEAD_DIM), F32), jax.ShapeDtypeStruct((N_HEADS, t), F32)),
        grid=(t // tm,),
        in_specs=[
            pl.BlockSpec((tm, d), lambda i: (i, 0)),
            pl.BlockSpec((HEAD_DIM, d), lambda i: (0, 0)),
            pl.BlockSpec((1, HEAD_DIM), lambda i: (0, 0)),
            pl.BlockSpec((1, HEAD_DIM), lambda i: (0, 0)),
        ],
        out_specs=(pl.BlockSpec((tm, HEAD_DIM), lambda i: (i, 0)), pl.BlockSpec((N_HEADS, tm), lambda i: (0, i))),
        compiler_params=_cparams(("parallel",)),
        name="gates",
    )(h, w, a, dtb)


ATT_SUB = 128
ATT_WIN = ATT_SUB + LEFT_CHUNKS * CHUNK


def _attn_bias_tile(rel_bias):
    period = ATT_SUB + ATT_WIN
    offs = np.arange(period)
    key_minus_query = np.where(offs < ATT_WIN, offs, offs - period)
    rel = LEFT_CHUNKS * CHUNK - key_minus_query
    diag_vals = rel_bias[:, np.clip(rel, -REL_CLIP, REL_CLIP) + REL_CLIP].astype(F32)
    flat = jnp.tile(diag_vals, (1, ATT_SUB))[:, :ATT_SUB * (period - 1)]
    bias = flat.reshape(-1, ATT_SUB, period - 1)[:, :, :ATT_WIN]
    qc = np.arange(ATT_SUB)[:, None] // CHUNK
    kc = np.arange(ATT_WIN)[None, :] // CHUNK
    allowed = (kc >= qc) & (kc <= qc + LEFT_CHUNKS)
    return jnp.where(allowed[None], bias, NEG)


def _head_rmsnorm(x, w):
    xf = x.astype(F32)
    return xf * lax.rsqrt(jnp.mean(xf * xf, axis=-1, keepdims=True) + EPS) * w


def _attn_body(q_ref, kp_ref, kc_ref, vp_ref, vc_ref, bias_ref, qw_ref, kw_ref, o_ref):
    t = pl.program_id(1)
    q = _head_rmsnorm(q_ref[...], qw_ref[...]).astype(BF16)
    k = jnp.concatenate(
        [_head_rmsnorm(kp_ref[...], kw_ref[...]), _head_rmsnorm(kc_ref[...], kw_ref[...])], axis=0
    ).astype(BF16)
    v = jnp.concatenate([vp_ref[...], vc_ref[...]], axis=0)
    n_sub = SEQ_TILE // ATT_SUB
    qb = q.reshape(n_sub, ATT_SUB, HEAD_DIM)
    kb = jnp.stack([k[p * ATT_SUB:p * ATT_SUB + ATT_WIN] for p in range(n_sub)], axis=0)
    vb = jnp.stack([v[p * ATT_SUB:p * ATT_SUB + ATT_WIN] for p in range(n_sub)], axis=0)
    s = jnp.einsum("pqd,pkd->pqk", qb, kb, preferred_element_type=F32) + bias_ref[...][None]
    col = lax.broadcasted_iota(jnp.int32, (n_sub, ATT_SUB, ATT_WIN), 2)
    sub = lax.broadcasted_iota(jnp.int32, (n_sub, ATT_SUB, ATT_WIN), 0)
    first_valid = jnp.where(t > 0, 0, SEQ_TILE) - sub * jnp.where(t > 0, 0, ATT_SUB)
    s = jnp.where(col >= first_valid, s, NEG)
    m = jnp.max(s, axis=-1, keepdims=True)
    e = jnp.exp(s - m)
    l = jnp.sum(e, axis=-1, keepdims=True)
    o = jnp.einsum("pqk,pkd->pqd", e.astype(BF16), vb, preferred_element_type=F32) / l
    o_ref[...] = o.reshape(SEQ_TILE, HEAD_DIM).astype(o_ref.dtype)


def _band_attention(p3, bias_tile, q_norm_w, k_norm_w, bsz, seq):
    nt = seq // SEQ_TILE
    qw = (q_norm_w.astype(F32) * (HEAD_DIM ** -0.5)).reshape(1, HEAD_DIM)
    kw = k_norm_w.astype(F32).reshape(1, HEAD_DIM)

    def cur(blk):
        return lambda bh, t: (blk + bh % N_HEADS, (bh // N_HEADS) * nt + t, 0)

    def prev(blk):
        return lambda bh, t: (blk + bh % N_HEADS, (bh // N_HEADS) * nt + jnp.maximum(t - 1, 0), 0)

    tile = (None, SEQ_TILE, HEAD_DIM)
    return pl.pallas_call(
        _attn_body,
        out_shape=jax.ShapeDtypeStruct((bsz * seq, WIDTH), BF16),
        grid=(bsz * N_HEADS, nt),
        in_specs=[
            pl.BlockSpec(tile, cur(BLK_ATT_Q)),
            pl.BlockSpec(tile, prev(BLK_ATT_K)),
            pl.BlockSpec(tile, cur(BLK_ATT_K)),
            pl.BlockSpec(tile, prev(BLK_ATT_V)),
            pl.BlockSpec(tile, cur(BLK_ATT_V)),
            pl.BlockSpec((None, ATT_SUB, ATT_WIN), lambda bh, t: (bh % N_HEADS, 0, 0)),
            pl.BlockSpec((1, HEAD_DIM), lambda bh, t: (0, 0)),
            pl.BlockSpec((1, HEAD_DIM), lambda bh, t: (0, 0)),
        ],
        out_specs=pl.BlockSpec((SEQ_TILE, HEAD_DIM), lambda bh, t: ((bh // N_HEADS) * nt + t, bh % N_HEADS)),
        compiler_params=_cparams(("parallel", "arbitrary")),
        name="band_attention",
    )(p3, p3, p3, p3, p3, bias_tile, qw, kw)


CONV_HALO = 16


def _short_conv_silu(cur, halo, w, t, cat_ref):
    cur = cur.astype(F32)
    cat_ref[0:CONV_HALO, :] = jnp.where(t > 0, halo.astype(F32), 0.0)
    cat_ref[CONV_HALO:, :] = cur
    y = cur * w[CONV_K - 1:CONV_K]
    for i in range(CONV_K - 1):
        shift = CONV_K - 1 - i
        y = y + cat_ref[CONV_HALO - shift:CONV_HALO - shift + SEQ_TILE, :] * w[i:i + 1]
    return y * jax.nn.sigmoid(y)


def _l2norm(x):
    return x * lax.rsqrt(jnp.sum(x * x, axis=-1, keepdims=True) + EPS)


GDN_HB = 8


def _bmm(a, b):
    return jnp.einsum("gmk,gkn->gmn", a.astype(BF16), b.astype(BF16), preferred_element_type=F32)


def _bmm_nt(a, b):
    return jnp.einsum("gmk,gnk->gmn", a.astype(BF16), b.astype(BF16), preferred_element_type=F32)


def _gdn_chunk_terms(q, k, v, beta, gc, grow):
    ri = lax.broadcasted_iota(jnp.int32, (1, CHUNK, CHUNK), 1)
    ci = lax.broadcasted_iota(jnp.int32, (1, CHUNK, CHUNK), 2)
    gamma = jnp.exp(jnp.where(ri >= ci, gc - grow, NEG))
    kb = k * beta
    kqk = _bmm_nt(jnp.concatenate([kb, q], axis=1), k)
    m = jnp.where(ri > ci, kqk[:, :CHUNK] * gamma, 0.0)
    att = kqk[:, CHUNK:] * gamma
    inv = (ri == ci).astype(F32) - m
    pw = _bmm(m, m)
    for _ in range(4):
        both = _bmm(jnp.concatenate([inv, pw], axis=1), pw)
        inv = inv + both[:, :CHUNK]
        pw = both[:, CHUNK:]
    inv = inv + _bmm(inv, pw)
    eg = jnp.exp(gc)
    g_last = gc[:, CHUNK - 1:CHUNK]
    sol = _bmm(inv, jnp.concatenate([v * beta, kb * eg], axis=2))
    att_uw = _bmm(att, sol)
    k_dec = k * jnp.exp(g_last - gc)
    kd_uw = _bmm(jnp.swapaxes(k_dec, 1, 2), sol)
    lhs = jnp.concatenate([q * eg - att_uw[:, :, HEAD_DIM:], kd_uw[:, :, HEAD_DIM:]], axis=1)
    return lhs, att_uw[:, :, :HEAD_DIM], kd_uw[:, :, :HEAD_DIM], jnp.exp(g_last)


def _gdn_body(q_ref, k_ref, v_ref, qh_ref, kh_ref, vh_ref, wq_ref, wk_ref, wv_ref, z_ref,
              gcol_ref, grow_ref, ow_ref, o_ref, state_ref, cat_ref):
    group = pl.program_id(0) % (N_HEADS // GDN_HB)
    t = pl.program_id(1)

    @pl.when(t == 0)
    def _():
        state_ref[...] = jnp.zeros_like(state_ref)

    gates = gcol_ref[...]
    lane = lax.broadcasted_iota(jnp.int32, gates.shape, 1)
    n_chunks = SEQ_TILE // CHUNK
    n_prob = GDN_HB * n_chunks

    qs, ks, vs, betas, gcs, grows = [], [], [], [], [], []
    for hh in range(GDN_HB):
        head = group * GDN_HB + hh
        qs.append(_l2norm(_short_conv_silu(q_ref[hh], qh_ref[hh], wq_ref[hh], t, cat_ref.at[3 * hh]))
                  * (HEAD_DIM ** -0.5))
        ks.append(_l2norm(_short_conv_silu(k_ref[hh], kh_ref[hh], wk_ref[hh], t, cat_ref.at[3 * hh + 1])))
        vs.append(_short_conv_silu(v_ref[hh], vh_ref[hh], wv_ref[hh], t, cat_ref.at[3 * hh + 2]))
        betas.append(jnp.sum(jnp.where(lane == head, gates, 0.0), axis=-1, keepdims=True))
        gcs.append(jnp.sum(jnp.where(lane == head + N_HEADS, gates, 0.0), axis=-1, keepdims=True))
        grow = grow_ref[hh]
        grows.extend(grow[:, c * CHUNK:(c + 1) * CHUNK] for c in range(n_chunks))

    def chunked(parts):
        return jnp.stack(parts, axis=0).reshape(n_prob, CHUNK, parts[0].shape[-1])

    lhs, o_intra, s_add, decay = _gdn_chunk_terms(
        chunked(qs), chunked(ks), chunked(vs), chunked(betas), chunked(gcs), jnp.stack(grows, axis=0))
    lhs = lhs.reshape(GDN_HB, n_chunks, CHUNK + HEAD_DIM, HEAD_DIM)
    o_intra = o_intra.reshape(GDN_HB, n_chunks, CHUNK, HEAD_DIM)
    s_add = s_add.reshape(GDN_HB, n_chunks, HEAD_DIM, HEAD_DIM)
    decay = decay.reshape(GDN_HB, n_chunks, 1, 1)

    state = state_ref[...]
    ow = ow_ref[...]
    for c in range(n_chunks):
        prod = _bmm(lhs[:, c], state)
        o_c = prod[:, :CHUNK] + o_intra[:, c]
        state = state * decay[:, c] - prod[:, CHUNK:] + s_add[:, c]
        o_n = o_c * lax.rsqrt(jnp.mean(o_c * o_c, axis=-1, keepdims=True) + EPS) * ow
        for hh in range(GDN_HB):
            zc = z_ref[hh, c * CHUNK:(c + 1) * CHUNK, :].astype(F32)
            o_ref[c * CHUNK:(c + 1) * CHUNK, hh * HEAD_DIM:(hh + 1) * HEAD_DIM] = (
                o_n[hh] * (zc * jax.nn.sigmoid(zc))).astype(o_ref.dtype)
    state_ref[...] = state


def _gated_delta(p3, conv_w, gcol, grow, o_norm_w, bsz, seq):
    nt = seq // SEQ_TILE
    halos_per_tile = SEQ_TILE // CONV_HALO
    n_groups = N_HEADS // GDN_HB
    cw = conv_w.astype(F32).reshape(CONV_K, 3 * N_HEADS, HEAD_DIM).transpose(1, 0, 2)
    grow3 = grow.reshape(N_HEADS, 1, bsz * seq)

    def cur(blk):
        return lambda bg, t: (blk // GDN_HB + bg % n_groups, (bg // n_groups) * nt + t, 0)

    def halo(blk):
        return lambda bg, t: (
            blk // GDN_HB + bg % n_groups, jnp.maximum(((bg // n_groups) * nt + t) * halos_per_tile - 1, 0), 0)

    def cwmap(off):
        return lambda bg, t: (off // GDN_HB + bg % n_groups, 0, 0)

    tile = (GDN_HB, SEQ_TILE, HEAD_DIM)
    htile = (GDN_HB, CONV_HALO, HEAD_DIM)
    wtile = (GDN_HB, CONV_K, HEAD_DIM)
    return pl.pallas_call(
        _gdn_body,
        out_shape=jax.ShapeDtypeStruct((bsz * seq, WIDTH), BF16),
        grid=(bsz * n_groups, nt),
        in_specs=[
            pl.BlockSpec(tile, cur(BLK_DN_Q)),
            pl.BlockSpec(tile, cur(BLK_DN_K)),
            pl.BlockSpec(tile, cur(BLK_DN_V)),
            pl.BlockSpec(htile, halo(BLK_DN_Q)),
            pl.BlockSpec(htile, halo(BLK_DN_K)),
            pl.BlockSpec(htile, halo(BLK_DN_V)),
            pl.BlockSpec(wtile, cwmap(0)),
            pl.BlockSpec(wtile, cwmap(N_HEADS)),
            pl.BlockSpec(wtile, cwmap(2 * N_HEADS)),
            pl.BlockSpec(tile, cur(BLK_DN_Z)),
            pl.BlockSpec((SEQ_TILE, HEAD_DIM), lambda bg, t: ((bg // n_groups) * nt + t, 0)),
            pl.BlockSpec((GDN_HB, 1, SEQ_TILE), lambda bg, t: (bg % n_groups, 0, (bg // n_groups) * nt + t)),
            pl.BlockSpec((1, HEAD_DIM), lambda bg, t: (0, 0)),
        ],
        out_specs=pl.BlockSpec(
            (SEQ_TILE, GDN_HB * HEAD_DIM), lambda bg, t: ((bg // n_groups) * nt + t, bg % n_groups)),
        scratch_shapes=[
            pltpu.VMEM((GDN_HB, HEAD_DIM, HEAD_DIM), F32),
            pltpu.VMEM((3 * GDN_HB, CONV_HALO + SEQ_TILE, HEAD_DIM), F32),
        ],
        compiler_params=_cparams(("parallel", "arbitrary")),
        name="gated_delta",
    )(p3, p3, p3, p3, p3, p3, cw, cw, cw, p3, gcol, grow3, o_norm_w.astype(F32).reshape(1, HEAD_DIM))


def _outproj_body(ya_ref, yb_ref, wa_ref, wb_ref, x_ref, o_ref):
    acc = jnp.dot(ya_ref[...], wa_ref[...].astype(BF16), preferred_element_type=F32)
    acc = acc + jnp.dot(yb_ref[...], wb_ref[...].astype(BF16), preferred_element_type=F32)
    o_ref[...] = x_ref[...] + acc


def _outproj(ya, yb, w_out, x2d):
    t, d = x2d.shape
    tm = min(1024, t)
    tn = min(512, d)
    return pl.pallas_call(
        _outproj_body,
        out_shape=jax.ShapeDtypeStruct((t, d), F32),
        grid=(t // tm, d // tn),
        in_specs=[
            pl.BlockSpec((tm, WIDTH), lambda i, j: (i, 0)),
            pl.BlockSpec((tm, WIDTH), lambda i, j: (i, 0)),
            pl.BlockSpec((WIDTH, tn), lambda i, j: (0, j)),
            pl.BlockSpec((WIDTH, tn), lambda i, j: (1, j)),
            pl.BlockSpec((tm, tn), lambda i, j: (i, j)),
        ],
        out_specs=pl.BlockSpec((tm, tn), lambda i, j: (i, j)),
        compiler_params=_cparams(("parallel", "arbitrary")),
        name="outproj",
    )(ya, yb, w_out, w_out, x2d)


def _router_body(x_ref, nw_ref, wr_ref, h_ref, r_ref):
    x = x_ref[...]
    h = x * lax.rsqrt(jnp.mean(x * x, axis=-1, keepdims=True) + EPS) * nw_ref[...]
    h_ref[...] = h
    w = wr_ref[...]
    h_hi, w_hi = h.astype(BF16), w.astype(BF16)
    h_lo = (h - h_hi.astype(F32)).astype(BF16)
    w_lo = (w - w_hi.astype(F32)).astype(BF16)
    logits = (jnp.dot(h_hi, w_hi, preferred_element_type=F32) + jnp.dot(h_hi, w_lo, preferred_element_type=F32)
              + jnp.dot(h_lo, w_hi, preferred_element_type=F32))
    lane = lax.broadcasted_iota(jnp.int32, logits.shape, 1).astype(F32)
    big = float(HEAD_DIM)

    def first_argmax(vals, vmax):
        return jnp.min(jnp.where(vals == vmax, lane, big), axis=-1, keepdims=True)

    gl = jnp.where(lane < N_GROUPS, logits, NEG)
    gmax = jnp.max(gl, axis=-1, keepdims=True)
    gidx = first_argmax(gl, gmax)
    grp_w = 1.0 / jnp.sum(jnp.exp(gl - gmax), axis=-1, keepdims=True)
    lo = N_GROUPS + gidx * EXPERTS_PER_GROUP
    el = jnp.where((lane >= lo) & (lane < lo + EXPERTS_PER_GROUP), logits, NEG)
    m1 = jnp.max(el, axis=-1, keepdims=True)
    i1 = first_argmax(el, m1)
    el2 = jnp.where(lane == i1, NEG, el)
    m2 = jnp.max(el2, axis=-1, keepdims=True)
    i2 = first_argmax(el2, m2)
    r = jnp.exp(m2 - m1)
    w1 = 1.0 / (1.0 + r)
    w2 = r / (1.0 + r)
    out = jnp.where(lane == 0, i1 - N_GROUPS, 0.0)
    out = jnp.where(lane == 1, i2 - N_GROUPS, out)
    out = jnp.where(lane == 2, grp_w * w1, out)
    out = jnp.where(lane == 3, grp_w * w2, out)
    r_ref[...] = out


def _norm_router(x1, norm_w, w_group, w_router):
    t, d = x1.shape
    tm = min(256, t)
    wr = jnp.concatenate([w_group, w_router], axis=1).astype(F32)
    wr = jnp.pad(wr, ((0, 0), (0, HEAD_DIM - wr.shape[1])))
    return pl.pallas_call(
        _router_body,
        out_shape=(jax.ShapeDtypeStruct((t, d), F32), jax.ShapeDtypeStruct((t, HEAD_DIM), F32)),
        grid=(t // tm,),
        in_specs=[
            pl.BlockSpec((tm, d), lambda i: (i, 0)),
            pl.BlockSpec((1, d), lambda i: (0, 0)),
            pl.BlockSpec((d, HEAD_DIM), lambda i: (0, 0)),
        ],
        out_specs=(pl.BlockSpec((tm, d), lambda i: (i, 0)), pl.BlockSpec((tm, HEAD_DIM), lambda i: (i, 0))),
        compiler_params=_cparams(("parallel",)),
        name="norm_router",
    )(x1, norm_w.astype(F32).reshape(1, d), wr)


def _row_gather_start(src_hbm, idx_ref, dst, sem, row0, n_rows):
    def body(r, carry):
        rr = row0 + r
        pltpu.make_async_copy(src_hbm.at[pl.ds(idx_ref[0, rr], 1)], dst.at[pl.ds(rr, 1)], sem).start()
        return carry

    lax.fori_loop(0, n_rows, body, 0, unroll=8)


def _row_gather_wait(dst, sem):
    pltpu.make_async_copy(dst, dst, sem).wait()


MOE_BM = 256
FFN_CAST_CHUNKS = 8


def _cast_rows(src, dst, dst_row0=0):
    rows = src.shape[0] // FFN_CAST_CHUNKS
    for c in range(FFN_CAST_CHUNKS):
        dst[dst_row0 + c * rows:dst_row0 + (c + 1) * rows, :] = src[c * rows:(c + 1) * rows, :].astype(BF16)


def _ffn_body(nblk_ref, blke_ref, nxte_ref, tok_ref, tokn_ref, x_hbm, wg_hbm, wu_hbm, wd_hbm, y_ref,
              xbuf, wg_f32, wu_f32, wd_half, wg_bf, wu_bf, wd_bf, sem, wsem):
    i = pl.program_id(0)
    nblk = nblk_ref[0]
    slot = i % 2
    half = wd_half.shape[0]

    def gate_up_copies(e):
        return (pltpu.make_async_copy(wg_hbm.at[e], wg_f32, wsem.at[0]),
                pltpu.make_async_copy(wu_hbm.at[e], wu_f32, wsem.at[1]))

    def down_half_copy(e, h):
        return pltpu.make_async_copy(wd_hbm.at[e, pl.ds(h * half, half)], wd_half, wsem.at[2])

    def request_weights(e):
        for c in gate_up_copies(e):
            c.start()
        down_half_copy(e, 0).start()

    @pl.when((i == 0) & (nblk > 0))
    def _():
        _row_gather_start(x_hbm, tok_ref, xbuf.at[0], sem.at[0], 0, MOE_BM)
        request_weights(blke_ref[0])

    @pl.when(i + 1 < nblk)
    def _():
        _row_gather_start(x_hbm, tokn_ref, xbuf.at[1 - slot], sem.at[1 - slot], 0, MOE_BM)

    new_expert = (i == 0) | (blke_ref[i] != blke_ref[jnp.maximum(i - 1, 0)])

    @pl.when((i < nblk) & new_expert)
    def _():
        e = blke_ref[i]
        down_half_copy(e, 0).wait()
        _cast_rows(wd_half, wd_bf, 0)
        down_half_copy(e, 1).start()
        for c in gate_up_copies(e):
            c.wait()
        _cast_rows(wg_f32, wg_bf)
        _cast_rows(wu_f32, wu_bf)
        down_half_copy(e, 1).wait()
        _cast_rows(wd_half, wd_bf, half)

        @pl.when(nxte_ref[i] >= 0)
        def _():
            request_weights(nxte_ref[i])

    @pl.when(i < nblk)
    def _():
        _row_gather_wait(xbuf.at[slot], sem.at[slot])
        x = xbuf[slot].astype(BF16)
        g = jnp.dot(x, wg_bf[...], preferred_element_type=F32)
        u = jnp.dot(x, wu_bf[...], preferred_element_type=F32)
        hid = (g * jax.nn.sigmoid(g) * u).astype(BF16)
        y_ref[...] = jnp.dot(hid, wd_bf[...], preferred_element_type=F32)

    @pl.when(i >= nblk)
    def _():
        y_ref[...] = jnp.zeros_like(y_ref)


def _expert_ffn(h2, w_gate, w_up, w_down, nblk, blk_e, nxt_e, slot_tok):
    t, d = h2.shape
    n_blocks = blk_e.shape[0]
    tok_arr = slot_tok.reshape(n_blocks, 1, MOE_BM)
    smem_blk = functools.partial(pl.BlockSpec, (None, 1, MOE_BM), memory_space=pltpu.SMEM)
    hbm = pl.BlockSpec(memory_space=pl.ANY)
    grid_spec = pltpu.PrefetchScalarGridSpec(
        num_scalar_prefetch=3,
        grid=(n_blocks,),
        in_specs=[
            smem_blk(lambda i, nb, be, ne: (i, 0, 0)),
            smem_blk(lambda i, nb, be, ne: (jnp.minimum(i + 1, n_blocks - 1), 0, 0)),
            hbm, hbm, hbm, hbm,
        ],
        out_specs=pl.BlockSpec((MOE_BM, d), lambda i, nb, be, ne: (i, 0)),
        scratch_shapes=[
            pltpu.VMEM((2, MOE_BM, d), F32),
            pltpu.VMEM((d, D_EXPERT), F32),
            pltpu.VMEM((d, D_EXPERT), F32),
            pltpu.VMEM((D_EXPERT // 2, d), F32),
            pltpu.VMEM((d, D_EXPERT), BF16),
            pltpu.VMEM((d, D_EXPERT), BF16),
            pltpu.VMEM((D_EXPERT, d), BF16),
            pltpu.SemaphoreType.DMA((2,)),
            pltpu.SemaphoreType.DMA((3,)),
        ],
    )
    return pl.pallas_call(
        _ffn_body,
        out_shape=jax.ShapeDtypeStruct((n_blocks * MOE_BM, d), F32),
        grid_spec=grid_spec,
        compiler_params=_cparams(("arbitrary",)),
        name="expert_ffn",
    )(nblk, blk_e, nxt_e, tok_arr, tok_arr, h2, w_gate, w_up, w_down)


def _combine_body(d0_ref, d1_ref, d0n_ref, d1n_ref, x_ref, r_ref, y_hbm, o_ref, ybuf, sem):
    i = pl.program_id(0)
    n = pl.num_programs(0)
    slot = i % 2

    @pl.when(i == 0)
    def _():
        _row_gather_start(y_hbm, d0_ref, ybuf.at[0, 0], sem.at[0], 0, COMBINE_TM)
        _row_gather_start(y_hbm, d1_ref, ybuf.at[0, 1], sem.at[0], 0, COMBINE_TM)

    @pl.when(i + 1 < n)
    def _():
        _row_gather_start(y_hbm, d0n_ref, ybuf.at[1 - slot, 0], sem.at[1 - slot], 0, COMBINE_TM)
        _row_gather_start(y_hbm, d1n_ref, ybuf.at[1 - slot, 1], sem.at[1 - slot], 0, COMBINE_TM)

    _row_gather_wait(ybuf.at[slot], sem.at[slot])
    route = r_ref[...]
    o_ref[...] = x_ref[...] + route[:, 2:3] * ybuf[slot, 0] + route[:, 3:4] * ybuf[slot, 1]


def _combine(x1, route, y, dest0, dest1):
    t, d = x1.shape
    tm = min(COMBINE_TM, t)
    assert tm == COMBINE_TM
    n = t // tm
    d0 = dest0.reshape(n, 1, tm)
    d1 = dest1.reshape(n, 1, tm)
    smem_blk = functools.partial(pl.BlockSpec, (None, 1, tm), memory_space=pltpu.SMEM)
    nxt = lambda i: (jnp.minimum(i + 1, n - 1), 0, 0)
    return pl.pallas_call(
        _combine_body,
        out_shape=jax.ShapeDtypeStruct((t, d), F32),
        grid=(n,),
        in_specs=[
            smem_blk(lambda i: (i, 0, 0)),
            smem_blk(lambda i: (i, 0, 0)),
            smem_blk(nxt),
            smem_blk(nxt),
            pl.BlockSpec((tm, d), lambda i: (i, 0)),
            pl.BlockSpec((tm, HEAD_DIM), lambda i: (i, 0)),
            pl.BlockSpec(memory_space=pl.ANY),
        ],
        out_specs=pl.BlockSpec((tm, d), lambda i: (i, 0)),
        scratch_shapes=[pltpu.VMEM((2, 2, tm, d), F32), pltpu.SemaphoreType.DMA((2,))],
        compiler_params=_cparams(("arbitrary",)),
        name="moe_combine",
    )(d0, d1, d0, d1, x1, route, y)


def _dispatch_plan(route, t):
    expert = route[:, 0:2].astype(jnp.int32)
    n_assign = 2 * t
    flat_e = expert.reshape(n_assign)
    order = jnp.argsort(flat_e).astype(jnp.int32)
    below = jnp.sum(flat_e[None, :] < jnp.arange(N_EXPERTS + 1, dtype=jnp.int32)[:, None], axis=1)
    start = below[:-1].astype(jnp.int32)
    counts = (below[1:] - below[:-1]).astype(jnp.int32)
    padded = (counts + MOE_BM - 1) // MOE_BM * MOE_BM
    pad_end = jnp.cumsum(padded)
    pad_start = pad_end - padded
    shift = pad_start - start
    shift_step = shift - jnp.concatenate([jnp.zeros((1,), jnp.int32), shift[:-1]])
    pos = jnp.arange(n_assign, dtype=jnp.int32)
    dest_sorted = pos + jnp.sum(jnp.where(pos[:, None] >= start[None, :], shift_step[None, :], 0), axis=1)
    _, dest = lax.sort_key_val(order, dest_sorted)
    dest = dest.reshape(t, 2)
    n_blocks = n_assign // MOE_BM + N_EXPERTS
    nblk = (pad_end[-1] // MOE_BM).astype(jnp.int32).reshape(1)
    blk_start = jnp.arange(n_blocks, dtype=jnp.int32) * MOE_BM
    blk_e = jnp.minimum(
        jnp.sum(pad_end[None, :] <= blk_start[:, None], axis=1), N_EXPERTS - 1).astype(jnp.int32)
    off = (blk_start - pad_start[blk_e])[:, None] + jnp.arange(MOE_BM, dtype=jnp.int32)[None, :]
    src = jnp.clip(start[blk_e][:, None] + off, 0, n_assign - 1).reshape(-1)
    valid = (off < counts[blk_e][:, None]).reshape(-1)
    slot_tok = jnp.where(valid, order[src] // 2, 0).astype(jnp.int32)
    run_end = pad_end[blk_e] // MOE_BM
    nxt_e = jnp.where(run_end < nblk[0], blk_e[jnp.minimum(run_end, n_blocks - 1)], -1).astype(jnp.int32)
    last_e = blk_e[jnp.maximum(nblk[0] - 1, 0)]
    blk_e = jnp.where(jnp.arange(n_blocks) < nblk[0], blk_e, last_e)
    return nblk, blk_e, nxt_e, slot_tok, dest[:, 0], dest[:, 1]


def _layer(x, norm1_w, w_in, q_norm_w, k_norm_w, rel_bias, conv_w, a_log, dt_bias,
           o_norm_w, w_out, norm2_w, w_group, w_router, w_gate, w_up, w_down):
    bsz, seq, d = x.shape
    t = bsz * seq
    assert seq % SEQ_TILE == 0 and d % HEAD_DIM == 0 and t % COMBINE_TM == 0
    assert w_in.shape[1] == MAIN_COLS + 2 * N_HEADS
    x2d = x.reshape(t, d)

    w_in_t = jnp.swapaxes(w_in, 0, 1)
    h1 = _rmsnorm(x2d, norm1_w.astype(F32), BF16)
    p3 = _inproj(h1, w_in_t)
    gcol, grow = _gates(h1, w_in_t[MAIN_COLS:], a_log, dt_bias)

    ya = _band_attention(p3, _attn_bias_tile(rel_bias), q_norm_w, k_norm_w, bsz, seq)
    yb = _gated_delta(p3, conv_w, gcol, grow, o_norm_w, bsz, seq)
    x1 = _outproj(ya, yb, w_out, x2d)

    h2, route = _norm_router(x1, norm2_w, w_group, w_router)
    nblk, blk_e, nxt_e, slot_tok, dest0, dest1 = _dispatch_plan(route, t)
    y = _expert_ffn(h2, w_gate, w_up, w_down, nblk, blk_e, nxt_e, slot_tok)
    out = _combine(x1, route, y, dest0, dest1)
    return out.reshape(bsz, seq, d)


def kernel(x, norm1_w, w_in, q_norm_w, k_norm_w, rel_bias, conv_w, a_log, dt_bias, o_norm_w, w_out,
           norm2_w, w_group, w_router, w_gate, w_up, w_down):
    for l in range(norm1_w.shape[0]):
        x = _layer(x, norm1_w[l], w_in[l], q_norm_w[l], k_norm_w[l], rel_bias[l], conv_w[l], a_log[l],
                   dt_bias[l], o_norm_w[l], w_out[l], norm2_w[l], w_group[l], w_router[l], w_gate[l],
                   w_up[l], w_down[l])
    return x
```

```python
import functools

import jax
import jax.numpy as jnp
import numpy as np
from jax import lax
from jax.experimental import pallas as pl
from jax.experimental.pallas import tpu as pltpu

F32 = jnp.float32
BF16 = jnp.bfloat16

CHUNK = 64
HEAD_DIM = 128
N_HEADS = 16
WIDTH = N_HEADS * HEAD_DIM
LEFT_CHUNKS = 8
REL_CLIP = 256
CONV_K = 4
N_GROUPS = 8
EXPERTS_PER_GROUP = 8
N_EXPERTS = N_GROUPS * EXPERTS_PER_GROUP
D_EXPERT = 512
EPS = 1e-6
NEG = -1e30

BLK_ATT_Q, BLK_ATT_K, BLK_ATT_V = 0, 16, 32
BLK_DN_Q, BLK_DN_K, BLK_DN_V, BLK_DN_Z = 48, 64, 80, 96
N_MAIN_BLKS = 112
MAIN_COLS = N_MAIN_BLKS * HEAD_DIM

SEQ_TILE = 512
COMBINE_TM = 256

V7X_VMEM_LIMIT = 56 * 1024 * 1024


def _cparams(sem, vmem=V7X_VMEM_LIMIT):
    return pltpu.CompilerParams(dimension_semantics=sem, vmem_limit_bytes=vmem)


def _dot_nt(a, b):
    return lax.dot_general(a, b, (((1,), (1,)), ((), ())), preferred_element_type=F32)


def _rmsnorm_body(x_ref, w_ref, o_ref):
    x = x_ref[...]
    ms = jnp.mean(x * x, axis=-1, keepdims=True)
    o_ref[...] = (x * lax.rsqrt(ms + EPS) * w_ref[...]).astype(o_ref.dtype)


def _rmsnorm(x2d, w, out_dtype):
    t, d = x2d.shape
    tm = min(256, t)
    return pl.pallas_call(
        _rmsnorm_body,
        out_shape=jax.ShapeDtypeStruct((t, d), out_dtype),
        grid=(t // tm,),
        in_specs=[pl.BlockSpec((tm, d), lambda i: (i, 0)), pl.BlockSpec((1, d), lambda i: (0, 0))],
        out_specs=pl.BlockSpec((tm, d), lambda i: (i, 0)),
        compiler_params=_cparams(("parallel",)),
        name="rmsnorm",
    )(x2d, w.reshape(1, d))


def _inproj_body(h_ref, wt_ref, o_ref):
    acc = _dot_nt(h_ref[...], wt_ref[...].astype(BF16))
    for c in range(o_ref.shape[0]):
        o_ref[c] = acc[:, c * HEAD_DIM:(c + 1) * HEAD_DIM].astype(o_ref.dtype)


def _inproj(h, w_in_t):
    t, d = h.shape
    tm = min(1024, t)
    tn = 512
    nb = tn // HEAD_DIM
    return pl.pallas_call(
        _inproj_body,
        out_shape=jax.ShapeDtypeStruct((N_MAIN_BLKS, t, HEAD_DIM), BF16),
        grid=(t // tm, MAIN_COLS // tn),
        in_specs=[pl.BlockSpec((tm, d), lambda i, j: (i, 0)), pl.BlockSpec((tn, d), lambda i, j: (j, 0))],
        out_specs=pl.BlockSpec((nb, tm, HEAD_DIM), lambda i, j: (j, i, 0)),
        compiler_params=_cparams(("parallel", "arbitrary")),
        name="inproj",
    )(h, w_in_t)


def _chunk_cumsum_rows(x):
    row = lax.broadcasted_iota(jnp.int32, x.shape, 0) % CHUNK
    shift = 1
    while shift < CHUNK:
        x = x + jnp.where(row >= shift, pltpu.roll(x, shift, 0), 0.0)
        shift *= 2
    return x


def _gates_body(h_ref, wt_ref, a_ref, dtb_ref, col_ref, row_ref):
    p = _dot_nt(h_ref[...], wt_ref[...].astype(BF16))
    lane = lax.broadcasted_iota(jnp.int32, p.shape, 1)
    beta = jax.nn.sigmoid(p)
    z = p + dtb_ref[...]
    softplus = jnp.maximum(z, 0.0) + jnp.log1p(jnp.exp(-jnp.abs(z)))
    g = -a_ref[...] * softplus
    out = jnp.where(lane < N_HEADS, beta, _chunk_cumsum_rows(g))
    col_ref[...] = out
    row_ref[...] = out.T[N_HEADS:2 * N_HEADS, :]


def _gates(h, w_small_t, a_log, dt_bias):
    t, d = h.shape
    tm = min(1024, t)
    pad = HEAD_DIM - 2 * N_HEADS
    w = jnp.pad(w_small_t, ((0, pad), (0, 0)))
    a = jnp.pad(jnp.exp(a_log.astype(F32)), (N_HEADS, pad)).reshape(1, HEAD_DIM)
    dtb = jnp.pad(dt_bias.astype(F32), (N_HEADS, pad)).reshape(1, HEAD_DIM)
    return pl.pallas_call(
        _gates_body,
        out_shape=(jax.ShapeDtypeStruct((t, HEAD_DIM), F32), jax.ShapeDtypeStruct((N_HEADS, t), F32)),
        grid=(t // tm,),
        in_specs=[
            pl.BlockSpec((tm, d), lambda i: (i, 0)),
            pl.BlockSpec((HEAD_DIM, d), lambda i: (0, 0)),
            pl.BlockSpec((1, HEAD_DIM), lambda i: (0, 0)),
            pl.BlockSpec((1, HEAD_DIM), lambda i: (0, 0)),
        ],
        out_specs=(pl.BlockSpec((tm, HEAD_DIM), lambda i: (i, 0)), pl.BlockSpec((N_HEADS, tm), lambda i: (0, i))),
        compiler_params=_cparams(("parallel",)),
        name="gates",
    )(h, w, a, dtb)


ATT_SUB = 128
ATT_WIN = ATT_SUB + LEFT_CHUNKS * CHUNK
ATT_HB = 2


def _attn_bias_tile(rel_bias):
    period = ATT_SUB + ATT_WIN
    offs = np.arange(period)
    key_minus_query = np.where(offs < ATT_WIN, offs, offs - period)
    rel = LEFT_CHUNKS * CHUNK - key_minus_query
    diag_vals = rel_bias[:, np.clip(rel, -REL_CLIP, REL_CLIP) + REL_CLIP].astype(F32)
    flat = jnp.tile(diag_vals, (1, ATT_SUB))[:, :ATT_SUB * (period - 1)]
    bias = flat.reshape(-1, ATT_SUB, period - 1)[:, :, :ATT_WIN]
    qc = np.arange(ATT_SUB)[:, None] // CHUNK
    kc = np.arange(ATT_WIN)[None, :] // CHUNK
    allowed = (kc >= qc) & (kc <= qc + LEFT_CHUNKS)
    return jnp.where(allowed[None], bias, NEG)


def _head_rmsnorm(x, w):
    xf = x.astype(F32)
    return xf * lax.rsqrt(jnp.mean(xf * xf, axis=-1, keepdims=True) + EPS) * w


def _attn_body(q_ref, kp_ref, kc_ref, vp_ref, vc_ref, bias_ref, qw_ref, kw_ref, o_ref):
    t = pl.program_id(1)
    n_sub = SEQ_TILE // ATT_SUB
    qbs, kbs, vbs, biases = [], [], [], []
    for hh in range(ATT_HB):
        q = _head_rmsnorm(q_ref[hh], qw_ref[...]).astype(BF16)
        k = jnp.concatenate(
            [_head_rmsnorm(kp_ref[hh], kw_ref[...]), _head_rmsnorm(kc_ref[hh], kw_ref[...])], axis=0
        ).astype(BF16)
        v = jnp.concatenate([vp_ref[hh], vc_ref[hh]], axis=0)
        qbs.append(q.reshape(n_sub, ATT_SUB, HEAD_DIM))
        kbs.extend(k[p * ATT_SUB:p * ATT_SUB + ATT_WIN] for p in range(n_sub))
        vbs.extend(v[p * ATT_SUB:p * ATT_SUB + ATT_WIN] for p in range(n_sub))
        biases.append(jnp.broadcast_to(bias_ref[hh][None], (n_sub, ATT_SUB, ATT_WIN)))
    qb = jnp.concatenate(qbs, axis=0)
    kb = jnp.stack(kbs, axis=0)
    vb = jnp.stack(vbs, axis=0)
    s = jnp.einsum("pqd,pkd->pqk", qb, kb, preferred_element_type=F32) + jnp.concatenate(biases, axis=0)
    shape = (ATT_HB * n_sub, ATT_SUB, ATT_WIN)
    col = lax.broadcasted_iota(jnp.int32, shape, 2)
    sub = lax.broadcasted_iota(jnp.int32, shape, 0) % n_sub
    first_valid = jnp.where(t > 0, 0, SEQ_TILE) - sub * jnp.where(t > 0, 0, ATT_SUB)
    s = jnp.where(col >= first_valid, s, NEG)
    m = jnp.max(s, axis=-1, keepdims=True)
    e = jnp.exp(s - m)
    l = jnp.sum(e, axis=-1, keepdims=True)
    o = jnp.einsum("pqk,pkd->pqd", e.astype(BF16), vb, preferred_element_type=F32) / l
    for hh in range(ATT_HB):
        o_ref[:, hh * HEAD_DIM:(hh + 1) * HEAD_DIM] = o[hh * n_sub:(hh + 1) * n_sub].reshape(
            SEQ_TILE, HEAD_DIM).astype(o_ref.dtype)


def _band_attention(p3, bias_tile, q_norm_w, k_norm_w, bsz, seq):
    nt = seq // SEQ_TILE
    qw = (q_norm_w.astype(F32) * (HEAD_DIM ** -0.5)).reshape(1, HEAD_DIM)
    kw = k_norm_w.astype(F32).reshape(1, HEAD_DIM)

    ng = N_HEADS // ATT_HB

    def cur(blk):
        return lambda bg, t: (blk // ATT_HB + bg % ng, (bg // ng) * nt + t, 0)

    def prev(blk):
        return lambda bg, t: (blk // ATT_HB + bg % ng, (bg // ng) * nt + jnp.maximum(t - 1, 0), 0)

    tile = (ATT_HB, SEQ_TILE, HEAD_DIM)
    return pl.pallas_call(
        _attn_body,
        out_shape=jax.ShapeDtypeStruct((bsz * seq, WIDTH), BF16),
        grid=(bsz * ng, nt),
        in_specs=[
            pl.BlockSpec(tile, cur(BLK_ATT_Q)),
            pl.BlockSpec(tile, prev(BLK_ATT_K)),
            pl.BlockSpec(tile, cur(BLK_ATT_K)),
            pl.BlockSpec(tile, prev(BLK_ATT_V)),
            pl.BlockSpec(tile, cur(BLK_ATT_V)),
            pl.BlockSpec((ATT_HB, ATT_SUB, ATT_WIN), lambda bg, t: (bg % ng, 0, 0)),
            pl.BlockSpec((1, HEAD_DIM), lambda bg, t: (0, 0)),
            pl.BlockSpec((1, HEAD_DIM), lambda bg, t: (0, 0)),
        ],
        out_specs=pl.BlockSpec(
            (SEQ_TILE, ATT_HB * HEAD_DIM), lambda bg, t: ((bg // ng) * nt + t, bg % ng)),
        compiler_params=_cparams(("parallel", "arbitrary")),
        name="band_attention",
    )(p3, p3, p3, p3, p3, bias_tile, qw, kw)


CONV_HALO = 16


def _short_conv_silu(cur, halo, w, t, cat_ref):
    cur = cur.astype(F32)
    cat_ref[0:CONV_HALO, :] = jnp.where(t > 0, halo.astype(F32), 0.0)
    cat_ref[CONV_HALO:, :] = cur
    y = cur * w[CONV_K - 1:CONV_K]
    for i in range(CONV_K - 1):
        shift = CONV_K - 1 - i
        y = y + cat_ref[CONV_HALO - shift:CONV_HALO - shift + SEQ_TILE, :] * w[i:i + 1]
    return y * jax.nn.sigmoid(y)


def _l2norm(x):
    return x * lax.rsqrt(jnp.sum(x * x, axis=-1, keepdims=True) + EPS)


GDN_HB = 8


def _bmm(a, b):
    return jnp.einsum("gmk,gkn->gmn", a.astype(BF16), b.astype(BF16), preferred_element_type=F32)


def _bmm_nt(a, b):
    return jnp.einsum("gmk,gnk->gmn", a.astype(BF16), b.astype(BF16), preferred_element_type=F32)


def _gdn_chunk_terms(q, k, v, beta, gc, grow):
    ri = lax.broadcasted_iota(jnp.int32, (1, CHUNK, CHUNK), 1)
    ci = lax.broadcasted_iota(jnp.int32, (1, CHUNK, CHUNK), 2)
    gamma = jnp.exp(jnp.where(ri >= ci, gc - grow, NEG))
    kb = k * beta
    kqk = _bmm_nt(jnp.concatenate([kb, q], axis=1), k)
    m = jnp.where(ri > ci, kqk[:, :CHUNK] * gamma, 0.0)
    att = kqk[:, CHUNK:] * gamma
    inv = (ri == ci).astype(F32) - m
    pw = _bmm(m, m)
    for _ in range(4):
        both = _bmm(jnp.concatenate([inv, pw], axis=1), pw)
        inv = inv + both[:, :CHUNK]
        pw = both[:, CHUNK:]
    inv = inv + _bmm(inv, pw)
    eg = jnp.exp(gc)
    g_last = gc[:, CHUNK - 1:CHUNK]
    sol = _bmm(inv, jnp.concatenate([v * beta, kb * eg], axis=2))
    att_uw = _bmm(att, sol)
    k_dec = k * jnp.exp(g_last - gc)
    kd_uw = _bmm(jnp.swapaxes(k_dec, 1, 2), sol)
    lhs = jnp.concatenate([q * eg - att_uw[:, :, HEAD_DIM:], kd_uw[:, :, HEAD_DIM:]], axis=1)
    return lhs, att_uw[:, :, :HEAD_DIM], kd_uw[:, :, :HEAD_DIM], jnp.exp(g_last)


def _gdn_body(q_ref, k_ref, v_ref, qh_ref, kh_ref, vh_ref, wq_ref, wk_ref, wv_ref, z_ref,
              gcol_ref, grow_ref, ow_ref, o_ref, state_ref, cat_ref):
    group = pl.program_id(0) % (N_HEADS // GDN_HB)
    t = pl.program_id(1)

    @pl.when(t == 0)
    def _():
        state_ref[...] = jnp.zeros_like(state_ref)

    gates = gcol_ref[...]
    lane = lax.broadcasted_iota(jnp.int32, gates.shape, 1)
    n_chunks = SEQ_TILE // CHUNK
    n_prob = GDN_HB * n_chunks

    qs, ks, vs, betas, gcs, grows = [], [], [], [], [], []
    for hh in range(GDN_HB):
        head = group * GDN_HB + hh
        qs.append(_l2norm(_short_conv_silu(q_ref[hh], qh_ref[hh], wq_ref[hh], t, cat_ref.at[3 * hh]))
                  * (HEAD_DIM ** -0.5))
        ks.append(_l2norm(_short_conv_silu(k_ref[hh], kh_ref[hh], wk_ref[hh], t, cat_ref.at[3 * hh + 1])))
        vs.append(_short_conv_silu(v_ref[hh], vh_ref[hh], wv_ref[hh], t, cat_ref.at[3 * hh + 2]))
        betas.append(jnp.sum(jnp.where(lane == head, gates, 0.0), axis=-1, keepdims=True))
        gcs.append(jnp.sum(jnp.where(lane == head + N_HEADS, gates, 0.0), axis=-1, keepdims=True))
        grow = grow_ref[hh]
        grows.extend(grow[:, c * CHUNK:(c + 1) * CHUNK] for c in range(n_chunks))

    def chunked(parts):
        return jnp.stack(parts, axis=0).reshape(n_prob, CHUNK, parts[0].shape[-1])

    lhs, o_intra, s_add, decay = _gdn_chunk_terms(
        chunked(qs), chunked(ks), chunked(vs), chunked(betas), chunked(gcs), jnp.stack(grows, axis=0))
    lhs = lhs.reshape(GDN_HB, n_chunks, CHUNK + HEAD_DIM, HEAD_DIM)
    o_intra = o_intra.reshape(GDN_HB, n_chunks, CHUNK, HEAD_DIM)
    s_add = s_add.reshape(GDN_HB, n_chunks, HEAD_DIM, HEAD_DIM)
    decay = decay.reshape(GDN_HB, n_chunks, 1, 1)

    state = state_ref[...]
    ow = ow_ref[...]
    for c in range(n_chunks):
        prod = _bmm(lhs[:, c], state)
        o_c = prod[:, :CHUNK] + o_intra[:, c]
        state = state * decay[:, c] - prod[:, CHUNK:] + s_add[:, c]
        o_n = o_c * lax.rsqrt(jnp.mean(o_c * o_c, axis=-1, keepdims=True) + EPS) * ow
        for hh in range(GDN_HB):
            zc = z_ref[hh, c * CHUNK:(c + 1) * CHUNK, :].astype(F32)
            o_ref[c * CHUNK:(c + 1) * CHUNK, hh * HEAD_DIM:(hh + 1) * HEAD_DIM] = (
                o_n[hh] * (zc * jax.nn.sigmoid(zc))).astype(o_ref.dtype)
    state_ref[...] = state


def _gated_delta(p3, conv_w, gcol, grow, o_norm_w, bsz, seq):
    nt = seq // SEQ_TILE
    halos_per_tile = SEQ_TILE // CONV_HALO
    n_groups = N_HEADS // GDN_HB
    cw = conv_w.astype(F32).reshape(CONV_K, 3 * N_HEADS, HEAD_DIM).transpose(1, 0, 2)
    grow3 = grow.reshape(N_HEADS, 1, bsz * seq)

    def cur(blk):
        return lambda bg, t: (blk // GDN_HB + bg % n_groups, (bg // n_groups) * nt + t, 0)

    def halo(blk):
        return lambda bg, t: (
            blk // GDN_HB + bg % n_groups, jnp.maximum(((bg // n_groups) * nt + t) * halos_per_tile - 1, 0), 0)

    def cwmap(off):
        return lambda bg, t: (off // GDN_HB + bg % n_groups, 0, 0)

    tile = (GDN_HB, SEQ_TILE, HEAD_DIM)
    htile = (GDN_HB, CONV_HALO, HEAD_DIM)
    wtile = (GDN_HB, CONV_K, HEAD_DIM)
    return pl.pallas_call(
        _gdn_body,
        out_shape=jax.ShapeDtypeStruct((bsz * seq, WIDTH), BF16),
        grid=(bsz * n_groups, nt),
        in_specs=[
            pl.BlockSpec(tile, cur(BLK_DN_Q)),
            pl.BlockSpec(tile, cur(BLK_DN_K)),
            pl.BlockSpec(tile, cur(BLK_DN_V)),
            pl.BlockSpec(htile, halo(BLK_DN_Q)),
            pl.BlockSpec(htile, halo(BLK_DN_K)),
            pl.BlockSpec(htile, halo(BLK_DN_V)),
            pl.BlockSpec(wtile, cwmap(0)),
            pl.BlockSpec(wtile, cwmap(N_HEADS)),
            pl.BlockSpec(wtile, cwmap(2 * N_HEADS)),
            pl.BlockSpec(tile, cur(BLK_DN_Z)),
            pl.BlockSpec((SEQ_TILE, HEAD_DIM), lambda bg, t: ((bg // n_groups) * nt + t, 0)),
            pl.BlockSpec((GDN_HB, 1, SEQ_TILE), lambda bg, t: (bg % n_groups, 0, (bg // n_groups) * nt + t)),
            pl.BlockSpec((1, HEAD_DIM), lambda bg, t: (0, 0)),
        ],
        out_specs=pl.BlockSpec(
            (SEQ_TILE, GDN_HB * HEAD_DIM), lambda bg, t: ((bg // n_groups) * nt + t, bg % n_groups)),
        scratch_shapes=[
            pltpu.VMEM((GDN_HB, HEAD_DIM, HEAD_DIM), F32),
            pltpu.VMEM((3 * GDN_HB, CONV_HALO + SEQ_TILE, HEAD_DIM), F32),
        ],
        compiler_params=_cparams(("parallel", "arbitrary")),
        name="gated_delta",
    )(p3, p3, p3, p3, p3, p3, cw, cw, cw, p3, gcol, grow3, o_norm_w.astype(F32).reshape(1, HEAD_DIM))


def _outproj_body(ya_ref, yb_ref, wa_ref, wb_ref, x_ref, o_ref):
    acc = jnp.dot(ya_ref[...], wa_ref[...].astype(BF16), preferred_element_type=F32)
    acc = acc + jnp.dot(yb_ref[...], wb_ref[...].astype(BF16), preferred_element_type=F32)
    o_ref[...] = x_ref[...] + acc


def _outproj(ya, yb, w_out, x2d):
    t, d = x2d.shape
    tm = min(1024, t)
    tn = min(512, d)
    return pl.pallas_call(
        _outproj_body,
        out_shape=jax.ShapeDtypeStruct((t, d), F32),
        grid=(t // tm, d // tn),
        in_specs=[
            pl.BlockSpec((tm, WIDTH), lambda i, j: (i, 0)),
            pl.BlockSpec((tm, WIDTH), lambda i, j: (i, 0)),
            pl.BlockSpec((WIDTH, tn), lambda i, j: (0, j)),
            pl.BlockSpec((WIDTH, tn), lambda i, j: (1, j)),
            pl.BlockSpec((tm, tn), lambda i, j: (i, j)),
        ],
        out_specs=pl.BlockSpec((tm, tn), lambda i, j: (i, j)),
        compiler_params=_cparams(("parallel", "arbitrary")),
        name="outproj",
    )(ya, yb, w_out, w_out, x2d)


def _router_body(x_ref, nw_ref, wr_ref, h_ref, r_ref):
    x = x_ref[...]
    h = x * lax.rsqrt(jnp.mean(x * x, axis=-1, keepdims=True) + EPS) * nw_ref[...]
    h_ref[...] = h
    w = wr_ref[...]
    h_hi, w_hi = h.astype(BF16), w.astype(BF16)
    h_lo = (h - h_hi.astype(F32)).astype(BF16)
    w_lo = (w - w_hi.astype(F32)).astype(BF16)
    logits = (jnp.dot(h_hi, w_hi, preferred_element_type=F32) + jnp.dot(h_hi, w_lo, preferred_element_type=F32)
              + jnp.dot(h_lo, w_hi, preferred_element_type=F32))
    lane = lax.broadcasted_iota(jnp.int32, logits.shape, 1).astype(F32)
    big = float(HEAD_DIM)

    def first_argmax(vals, vmax):
        return jnp.min(jnp.where(vals == vmax, lane, big), axis=-1, keepdims=True)

    gl = jnp.where(lane < N_GROUPS, logits, NEG)
    gmax = jnp.max(gl, axis=-1, keepdims=True)
    gidx = first_argmax(gl, gmax)
    grp_w = 1.0 / jnp.sum(jnp.exp(gl - gmax), axis=-1, keepdims=True)
    lo = N_GROUPS + gidx * EXPERTS_PER_GROUP
    el = jnp.where((lane >= lo) & (lane < lo + EXPERTS_PER_GROUP), logits, NEG)
    m1 = jnp.max(el, axis=-1, keepdims=True)
    i1 = first_argmax(el, m1)
    el2 = jnp.where(lane == i1, NEG, el)
    m2 = jnp.max(el2, axis=-1, keepdims=True)
    i2 = first_argmax(el2, m2)
    r = jnp.exp(m2 - m1)
    w1 = 1.0 / (1.0 + r)
    w2 = r / (1.0 + r)
    out = jnp.where(lane == 0, i1 - N_GROUPS, 0.0)
    out = jnp.where(lane == 1, i2 - N_GROUPS, out)
    out = jnp.where(lane == 2, grp_w * w1, out)
    out = jnp.where(lane == 3, grp_w * w2, out)
    r_ref[...] = out


def _norm_router(x1, norm_w, w_group, w_router):
    t, d = x1.shape
    tm = min(256, t)
    wr = jnp.concatenate([w_group, w_router], axis=1).astype(F32)
    wr = jnp.pad(wr, ((0, 0), (0, HEAD_DIM - wr.shape[1])))
    return pl.pallas_call(
        _router_body,
        out_shape=(jax.ShapeDtypeStruct((t, d), F32), jax.ShapeDtypeStruct((t, HEAD_DIM), F32)),
        grid=(t // tm,),
        in_specs=[
            pl.BlockSpec((tm, d), lambda i: (i, 0)),
            pl.BlockSpec((1, d), lambda i: (0, 0)),
            pl.BlockSpec((d, HEAD_DIM), lambda i: (0, 0)),
        ],
        out_specs=(pl.BlockSpec((tm, d), lambda i: (i, 0)), pl.BlockSpec((tm, HEAD_DIM), lambda i: (i, 0))),
        compiler_params=_cparams(("parallel",)),
        name="norm_router",
    )(x1, norm_w.astype(F32).reshape(1, d), wr)


def _row_gather_start(src_hbm, idx_ref, dst, sem, row0, n_rows):
    def body(r, carry):
        rr = row0 + r
        pltpu.make_async_copy(src_hbm.at[pl.ds(idx_ref[0, rr], 1)], dst.at[pl.ds(rr, 1)], sem).start()
        return carry

    lax.fori_loop(0, n_rows, body, 0, unroll=8)


def _row_gather_wait(dst, sem):
    pltpu.make_async_copy(dst, dst, sem).wait()


MOE_BM = 256
FFN_CAST_CHUNKS = 8


def _cast_rows(src, dst, dst_row0=0):
    rows = src.shape[0] // FFN_CAST_CHUNKS
    for c in range(FFN_CAST_CHUNKS):
        dst[dst_row0 + c * rows:dst_row0 + (c + 1) * rows, :] = src[c * rows:(c + 1) * rows, :].astype(BF16)


def _ffn_body(nblk_ref, blke_ref, nxte_ref, tok_ref, tokn_ref, x_hbm, wg_hbm, wu_hbm, wd_hbm, y_ref,
              xbuf, wg_f32, wu_f32, wd_half, wg_bf, wu_bf, wd_bf, sem, wsem):
    i = pl.program_id(0)
    nblk = nblk_ref[0]
    slot = i % 2
    half = wd_half.shape[0]

    def gate_up_copies(e):
        return (pltpu.make_async_copy(wg_hbm.at[e], wg_f32, wsem.at[0]),
                pltpu.make_async_copy(wu_hbm.at[e], wu_f32, wsem.at[1]))

    def down_half_copy(e, h):
        return pltpu.make_async_copy(wd_hbm.at[e, pl.ds(h * half, half)], wd_half, wsem.at[2])

    def request_weights(e):
        for c in gate_up_copies(e):
            c.start()
        down_half_copy(e, 0).start()

    @pl.when((i == 0) & (nblk > 0))
    def _():
        _row_gather_start(x_hbm, tok_ref, xbuf.at[0], sem.at[0], 0, MOE_BM)
        request_weights(blke_ref[0])

    @pl.when(i + 1 < nblk)
    def _():
        _row_gather_start(x_hbm, tokn_ref, xbuf.at[1 - slot], sem.at[1 - slot], 0, MOE_BM)

    new_expert = (i == 0) | (blke_ref[i] != blke_ref[jnp.maximum(i - 1, 0)])

    @pl.when((i < nblk) & new_expert)
    def _():
        e = blke_ref[i]
        down_half_copy(e, 0).wait()
        _cast_rows(wd_half, wd_bf, 0)
        down_half_copy(e, 1).start()
        for c in gate_up_copies(e):
            c.wait()
        _cast_rows(wg_f32, wg_bf)
        _cast_rows(wu_f32, wu_bf)
        down_half_copy(e, 1).wait()
        _cast_rows(wd_half, wd_bf, half)

        @pl.when(nxte_ref[i] >= 0)
        def _():
            request_weights(nxte_ref[i])

    @pl.when(i < nblk)
    def _():
        _row_gather_wait(xbuf.at[slot], sem.at[slot])
        x = xbuf[slot].astype(BF16)
        g = jnp.dot(x, wg_bf[...], preferred_element_type=F32)
        u = jnp.dot(x, wu_bf[...], preferred_element_type=F32)
        hid = (g * jax.nn.sigmoid(g) * u).astype(BF16)
        y_ref[...] = jnp.dot(hid, wd_bf[...], preferred_element_type=F32)

    @pl.when(i >= nblk)
    def _():
        y_ref[...] = jnp.zeros_like(y_ref)


def _expert_ffn(h2, w_gate, w_up, w_down, nblk, blk_e, nxt_e, slot_tok):
    t, d = h2.shape
    n_blocks = blk_e.shape[0]
    tok_arr = slot_tok.reshape(n_blocks, 1, MOE_BM)
    smem_blk = functools.partial(pl.BlockSpec, (None, 1, MOE_BM), memory_space=pltpu.SMEM)
    hbm = pl.BlockSpec(memory_space=pl.ANY)
    grid_spec = pltpu.PrefetchScalarGridSpec(
        num_scalar_prefetch=3,
        grid=(n_blocks,),
        in_specs=[
            smem_blk(lambda i, nb, be, ne: (i, 0, 0)),
            smem_blk(lambda i, nb, be, ne: (jnp.minimum(i + 1, n_blocks - 1), 0, 0)),
            hbm, hbm, hbm, hbm,
        ],
        out_specs=pl.BlockSpec((MOE_BM, d), lambda i, nb, be, ne: (i, 0)),
        scratch_shapes=[
            pltpu.VMEM((2, MOE_BM, d), F32),
            pltpu.VMEM((d, D_EXPERT), F32),
            pltpu.VMEM((d, D_EXPERT), F32),
            pltpu.VMEM((D_EXPERT // 2, d), F32),
            pltpu.VMEM((d, D_EXPERT), BF16),
            pltpu.VMEM((d, D_EXPERT), BF16),
            pltpu.VMEM((D_EXPERT, d), BF16),
            pltpu.SemaphoreType.DMA((2,)),
            pltpu.SemaphoreType.DMA((3,)),
        ],
    )
    return pl.pallas_call(
        _ffn_body,
        out_shape=jax.ShapeDtypeStruct((n_blocks * MOE_BM, d), F32),
        grid_spec=grid_spec,
        compiler_params=_cparams(("arbitrary",)),
        name="expert_ffn",
    )(nblk, blk_e, nxt_e, tok_arr, tok_arr, h2, w_gate, w_up, w_down)


def _combine_body(d0_ref, d1_ref, d0n_ref, d1n_ref, x_ref, r_ref, y_hbm, o_ref, ybuf, sem):
    i = pl.program_id(0)
    n = pl.num_programs(0)
    slot = i % 2

    @pl.when(i == 0)
    def _():
        _row_gather_start(y_hbm, d0_ref, ybuf.at[0, 0], sem.at[0], 0, COMBINE_TM)
        _row_gather_start(y_hbm, d1_ref, ybuf.at[0, 1], sem.at[0], 0, COMBINE_TM)

    @pl.when(i + 1 < n)
    def _():
        _row_gather_start(y_hbm, d0n_ref, ybuf.at[1 - slot, 0], sem.at[1 - slot], 0, COMBINE_TM)
        _row_gather_start(y_hbm, d1n_ref, ybuf.at[1 - slot, 1], sem.at[1 - slot], 0, COMBINE_TM)

    _row_gather_wait(ybuf.at[slot], sem.at[slot])
    route = r_ref[...]
    o_ref[...] = x_ref[...] + route[:, 2:3] * ybuf[slot, 0] + route[:, 3:4] * ybuf[slot, 1]


def _combine(x1, route, y, dest0, dest1):
    t, d = x1.shape
    tm = min(COMBINE_TM, t)
    assert tm == COMBINE_TM
    n = t // tm
    d0 = dest0.reshape(n, 1, tm)
    d1 = dest1.reshape(n, 1, tm)
    smem_blk = functools.partial(pl.BlockSpec, (None, 1, tm), memory_space=pltpu.SMEM)
    nxt = lambda i: (jnp.minimum(i + 1, n - 1), 0, 0)
    return pl.pallas_call(
        _combine_body,
        out_shape=jax.ShapeDtypeStruct((t, d), F32),
        grid=(n,),
        in_specs=[
            smem_blk(lambda i: (i, 0, 0)),
            smem_blk(lambda i: (i, 0, 0)),
            smem_blk(nxt),
            smem_blk(nxt),
            pl.BlockSpec((tm, d), lambda i: (i, 0)),
            pl.BlockSpec((tm, HEAD_DIM), lambda i: (i, 0)),
            pl.BlockSpec(memory_space=pl.ANY),
        ],
        out_specs=pl.BlockSpec((tm, d), lambda i: (i, 0)),
        scratch_shapes=[pltpu.VMEM((2, 2, tm, d), F32), pltpu.SemaphoreType.DMA((2,))],
        compiler_params=_cparams(("arbitrary",)),
        name="moe_combine",
    )(d0, d1, d0, d1, x1, route, y)


def _dispatch_plan(route, t):
    expert = route[:, 0:2].astype(jnp.int32)
    n_assign = 2 * t
    flat_e = expert.reshape(n_assign)
    order = jnp.argsort(flat_e).astype(jnp.int32)
    below = jnp.sum(flat_e[None, :] < jnp.arange(N_EXPERTS + 1, dtype=jnp.int32)[:, None], axis=1)
    start = below[:-1].astype(jnp.int32)
    counts = (below[1:] - below[:-1]).astype(jnp.int32)
    padded = (counts + MOE_BM - 1) // MOE_BM * MOE_BM
    pad_end = jnp.cumsum(padded)
    pad_start = pad_end - padded
    shift = pad_start - start
    shift_step = shift - jnp.concatenate([jnp.zeros((1,), jnp.int32), shift[:-1]])
    pos = jnp.arange(n_assign, dtype=jnp.int32)
    dest_sorted = pos + jnp.sum(jnp.where(pos[:, None] >= start[None, :], shift_step[None, :], 0), axis=1)
    _, dest = lax.sort_key_val(order, dest_sorted)
    dest = dest.reshape(t, 2)
    n_blocks = n_assign // MOE_BM + N_EXPERTS
    nblk = (pad_end[-1] // MOE_BM).astype(jnp.int32).reshape(1)
    blk_start = jnp.arange(n_blocks, dtype=jnp.int32) * MOE_BM
    blk_e = jnp.minimum(
        jnp.sum(pad_end[None, :] <= blk_start[:, None], axis=1), N_EXPERTS - 1).astype(jnp.int32)
    off = (blk_start - pad_start[blk_e])[:, None] + jnp.arange(MOE_BM, dtype=jnp.int32)[None, :]
    src = jnp.clip(start[blk_e][:, None] + off, 0, n_assign - 1).reshape(-1)
    valid = (off < counts[blk_e][:, None]).reshape(-1)
    slot_tok = jnp.where(valid, order[src] // 2, 0).astype(jnp.int32)
    run_end = pad_end[blk_e] // MOE_BM
    nxt_e = jnp.where(run_end < nblk[0], blk_e[jnp.minimum(run_end, n_blocks - 1)], -1).astype(jnp.int32)
    last_e = blk_e[jnp.maximum(nblk[0] - 1, 0)]
    blk_e = jnp.where(jnp.arange(n_blocks) < nblk[0], blk_e, last_e)
    return nblk, blk_e, nxt_e, slot_tok, dest[:, 0], dest[:, 1]


def _layer(x, norm1_w, w_in, q_norm_w, k_norm_w, rel_bias, conv_w, a_log, dt_bias,
           o_norm_w, w_out, norm2_w, w_group, w_router, w_gate, w_up, w_down):
    bsz, seq, d = x.shape
    t = bsz * seq
    assert seq % SEQ_TILE == 0 and d % HEAD_DIM == 0 and t % COMBINE_TM == 0
    assert w_in.shape[1] == MAIN_COLS + 2 * N_HEADS
    x2d = x.reshape(t, d)

    w_in_t = jnp.swapaxes(w_in, 0, 1)
    h1 = _rmsnorm(x2d, norm1_w.astype(F32), BF16)
    p3 = _inproj(h1, w_in_t)
    gcol, grow = _gates(h1, w_in_t[MAIN_COLS:], a_log, dt_bias)

    ya = _band_attention(p3, _attn_bias_tile(rel_bias), q_norm_w, k_norm_w, bsz, seq)
    yb = _gated_delta(p3, conv_w, gcol, grow, o_norm_w, bsz, seq)
    x1 = _outproj(ya, yb, w_out, x2d)

    h2, route = _norm_router(x1, norm2_w, w_group, w_router)
    nblk, blk_e, nxt_e, slot_tok, dest0, dest1 = _dispatch_plan(route, t)
    y = _expert_ffn(h2, w_gate, w_up, w_down, nblk, blk_e, nxt_e, slot_tok)
    out = _combine(x1, route, y, dest0, dest1)
    return out.reshape(bsz, seq, d)


def kernel(x, norm1_w, w_in, q_norm_w, k_norm_w, rel_bias, conv_w, a_log, dt_bias, o_norm_w, w_out,
           norm2_w, w_group, w_router, w_gate, w_up, w_down):
    for l in range(norm1_w.shape[0]):
        x = _layer(x, norm1_w[l], w_in[l], q_norm_w[l], k_norm_w[l], rel_bias[l], conv_w[l], a_log[l],
                   dt_bias[l], o_norm_w[l], w_out[l], norm2_w[l], w_group[l], w_router[l], w_gate[l],
                   w_up[l], w_down[l])
    return x
```
